```python
import jax, jax.numpy as jnp
from jax import lax
import numpy as np

D_MODEL = 4096
BATCH = 4
SEQ = 4096
DEPTH = 4

N_A = DEPTH // 2
N_B = DEPTH - N_A
ALPHA = (2.0 * DEPTH) ** 0.25
BETA = (8.0 * DEPTH) ** -0.25
CHUNK = 128
D_INNER_A = D_MODEL
GROUPS_A = D_INNER_A // 128
HEAD_DIM = 64
N_Q = D_MODEL // HEAD_DIM
N_KV = 8
GQA = N_Q // N_KV
WINDOW = 128
BLK = WINDOW
D_FF = 11008
CONV_W = 3
LN_EPS = 1e-5

kernel_name = "yoco_gmlp_swa_sink_convffn_deepnorm"


def layer_norm(x, g, b):
    xf = x.astype(jnp.float32)
    mu = jnp.mean(xf, axis=-1, keepdims=True)
    xc = xf - mu
    var = jnp.mean(xc * xc, axis=-1, keepdims=True)
    return (xc * lax.rsqrt(var + LN_EPS) * g.astype(jnp.float32) + b.astype(jnp.float32)).astype(x.dtype)


def conv_ffn(h, w_up, conv_w, conv_b, w_down):
    z = h @ w_up
    zp = jnp.pad(z, ((0, 0), (CONV_W - 1, 0), (0, 0)))
    z = conv_w[0] * zp[:, :-2] + conv_w[1] * zp[:, 1:-1] + conv_w[2] * zp[:, 2:] + conv_b
    g, u = jnp.split(z, 2, axis=-1)
    return (jax.nn.silu(g) * u) @ w_down


def gmlp_mixer(h, w_in, v_g, v_b, w_s, b_s, w_out):
    bsz, seq, _ = h.shape
    z = jax.nn.gelu(h @ w_in, approximate=False)
    u, v = jnp.split(z, 2, axis=-1)
    v = layer_norm(v, v_g, v_b)
    v = v.reshape(bsz, seq // CHUNK, CHUNK, GROUPS_A, D_INNER_A // GROUPS_A)
    causal = jnp.tril(jnp.ones((CHUNK, CHUNK), dtype=bool))
    w = jnp.where(causal[None], w_s, jnp.zeros((), w_s.dtype))
    vm = jnp.einsum('gts,bnsgc->bntgc', w, v) + b_s.T[None, None, :, :, None]
    return (u * vm.reshape(bsz, seq, D_INNER_A)) @ w_out


def shared_kv(h, w_kv):
    bsz, seq, _ = h.shape
    nb = seq // BLK
    kv = (h @ w_kv).reshape(bsz, seq, 2, N_KV, HEAD_DIM)

    def to_band(t):
        tb = t.reshape(bsz, nb, BLK, N_KV, HEAD_DIM)
        prev = jnp.pad(tb[:, :-1], ((0, 0), (1, 0), (0, 0), (0, 0), (0, 0)))
        band = jnp.concatenate([prev, tb], axis=2)
        return jnp.moveaxis(band, 1, 0)

    return to_band(kv[:, :, 0]), to_band(kv[:, :, 1])


def swa_sink_mixer(h, w_q, sinks, w_out, k_band, v_band):
    bsz, seq, _ = h.shape
    nb = seq // BLK
    scale = HEAD_DIM ** -0.5
    q = (h @ w_q).reshape(bsz, nb, BLK, N_KV, GQA, HEAD_DIM) * scale
    q = jnp.moveaxis(q, 1, 0)
    sink = sinks.astype(jnp.float32).reshape(N_KV, GQA)[:, :, None, None]
    qi = jnp.arange(BLK)[:, None]
    kj = jnp.arange(2 * BLK)[None, :]
    band_mask = (kj > qi) & (kj <= qi + WINDOW)

    def attend(args):
        qb, kb, vb, n = args
        s = jnp.einsum('bqhgd,bkhd->bhgqk', qb, kb).astype(jnp.float32)
        valid = band_mask & ((n > 0) | (kj >= BLK))
        s = jnp.where(valid, s, -jnp.inf)
        m = jnp.maximum(jnp.max(s, axis=-1, keepdims=True), sink)
        p = jnp.exp(s - m)
        denom = jnp.sum(p, axis=-1, keepdims=True) + jnp.exp(sink - m)
        return jnp.einsum('bhgqk,bkhd->bqhgd', (p / denom).astype(vb.dtype), vb)

    o = lax.map(attend, (q, k_band, v_band, jnp.arange(nb)))
    o = jnp.moveaxis(o, 0, 1).reshape(bsz, seq, N_Q * HEAD_DIM)
    return o @ w_out


def setup_inputs(seed: int = 0) -> dict:
    key = jax.random.key(seed)
    ks = jax.random.split(key, 17)

    def nrm(k, shape, scale):
        return jax.random.normal(k, shape, jnp.float32) * scale

    return {
        "x": nrm(ks[0], (BATCH, SEQ, D_MODEL), 1.0),
        "mix_in_a": nrm(ks[1], (N_A, D_MODEL, 2 * D_INNER_A), D_MODEL ** -0.5),
        "norm_v_a_g": 1.0 + nrm(ks[2], (N_A, D_INNER_A), 0.02),
        "norm_v_a_b": nrm(ks[3], (N_A, D_INNER_A), 0.02),
        "sgu_w": nrm(ks[4], (N_A, GROUPS_A, CHUNK, CHUNK), CHUNK ** -0.5),
        "sgu_b": 1.0 + nrm(ks[5], (N_A, GROUPS_A, CHUNK), 0.02),
        "mix_out_a": nrm(ks[6], (N_A, D_INNER_A, D_MODEL), D_INNER_A ** -0.5 * BETA),
        "w_kv": nrm(ks[7], (D_MODEL, 2 * N_KV * HEAD_DIM), D_MODEL ** -0.5),
        "mix_in_b": nrm(ks[8], (N_B, D_MODEL, N_Q * HEAD_DIM), D_MODEL ** -0.5),
        "sinks": nrm(ks[9], (N_B, N_Q), 0.5),
        "mix_out_b": nrm(ks[10], (N_B, N_Q * HEAD_DIM, D_MODEL), (N_Q * HEAD_DIM) ** -0.5 * BETA),
        "ffn_up": nrm(ks[11], (DEPTH, D_MODEL, 2 * D_FF), D_MODEL ** -0.5),
        "ffn_conv_w": nrm(ks[12], (DEPTH, CONV_W, 2 * D_FF), CONV_W ** -0.5),
        "ffn_conv_b": nrm(ks[13], (DEPTH, 2 * D_FF), 0.01),
        "ffn_down": nrm(ks[14], (DEPTH, D_FF, D_MODEL), D_FF ** -0.5 * BETA),
        "ln_g": 1.0 + nrm(ks[15], (DEPTH, 2, D_MODEL), 0.02),
        "ln_b": nrm(ks[16], (DEPTH, 2, D_MODEL), 0.02),
    }


def reference(x, mix_in_a, norm_v_a_g, norm_v_a_b, sgu_w, sgu_b, mix_out_a, w_kv,
              mix_in_b, sinks, mix_out_b, ffn_up, ffn_conv_w, ffn_conv_b, ffn_down,
              ln_g, ln_b):
    h = x
    k_band = v_band = None
    for l in range(DEPTH):
        if l < N_A:
            mix = gmlp_mixer(h, mix_in_a[l], norm_v_a_g[l], norm_v_a_b[l],
                             sgu_w[l], sgu_b[l], mix_out_a[l])
        else:
            if l == N_A:
                k_band, v_band = shared_kv(h, w_kv)
            j = l - N_A
            mix = swa_sink_mixer(h, mix_in_b[j], sinks[j], mix_out_b[j], k_band, v_band)
        h = layer_norm(ALPHA * h + mix, ln_g[l, 0], ln_b[l, 0])
        h = layer_norm(ALPHA * h + conv_ffn(h, ffn_up[l], ffn_conv_w[l], ffn_conv_b[l], ffn_down[l]),
                       ln_g[l, 1], ln_b[l, 1])
    return h
```

```python
import functools

import jax
import jax.numpy as jnp
from jax import lax
from jax.experimental import pallas as pl
from jax.experimental.pallas import tpu as pltpu

LN_EPS = 1e-5
CHUNK = 128
HEAD_DIM = 64
N_KV = 8
WINDOW = 128
CONV_W = 3

VMEM_LIMIT_BYTES = 56 * 1024 * 1024
SUBLANES = 8

_f32 = jnp.float32
_bf16 = jnp.bfloat16


def _params(sem):
    return pltpu.CompilerParams(dimension_semantics=sem, vmem_limit_bytes=VMEM_LIMIT_BYTES)


def _layer_norm(y, g, b):
    mu = jnp.mean(y, axis=-1, keepdims=True)
    yc = y - mu
    var = jnp.mean(yc * yc, axis=-1, keepdims=True)
    return yc * lax.rsqrt(var + LN_EPS) * g + b


def _mm_act_kernel(x_ref, w_ref, o_ref, *, act):
    z = jnp.dot(x_ref[...], w_ref[...], preferred_element_type=_f32)
    o_ref[...] = act(z).astype(o_ref.dtype)


def _mm_act(x, w, act, out_dtype, bm, bn):
    m, k = x.shape
    n = w.shape[1]
    return pl.pallas_call(
        functools.partial(_mm_act_kernel, act=act),
        grid=(m // bm, n // bn),
        in_specs=[pl.BlockSpec((bm, k), lambda i, j: (i, 0)),
                  pl.BlockSpec((k, bn), lambda i, j: (0, j))],
        out_specs=pl.BlockSpec((bm, bn), lambda i, j: (i, j)),
        out_shape=jax.ShapeDtypeStruct((m, n), out_dtype),
        compiler_params=_params(("arbitrary", "arbitrary")),
    )(x, w)


def _mm_res_ln_kernel(x_ref, w_ref, res_ref, g_ref, b_ref, o_ref, ob_ref, *, alpha, bn, nslab):
    j = pl.program_id(1)
    y = alpha * res_ref[...] + jnp.dot(x_ref[...], w_ref[...], preferred_element_type=_f32)
    for s in range(nslab):
        @pl.when(j == s)
        def _():
            o_ref[:, s * bn:(s + 1) * bn] = y

    @pl.when(j == nslab - 1)
    def _():
        h = _layer_norm(o_ref[...], g_ref[...], b_ref[...])
        o_ref[...] = h
        ob_ref[...] = h.astype(_bf16)


def _mm_res_ln(x, w, res, g, b, alpha, bm, bn):
    m, k = x.shape
    n = w.shape[1]
    nslab = n // bn
    return pl.pallas_call(
        functools.partial(_mm_res_ln_kernel, alpha=alpha, bn=bn, nslab=nslab),
        grid=(m // bm, nslab),
        in_specs=[pl.BlockSpec((bm, k), lambda i, j: (i, 0)),
                  pl.BlockSpec((k, bn), lambda i, j: (0, j)),
                  pl.BlockSpec((bm, bn), lambda i, j: (i, j)),
                  pl.BlockSpec((1, n), lambda i, j: (0, 0)),
                  pl.BlockSpec((1, n), lambda i, j: (0, 0))],
        out_specs=[pl.BlockSpec((bm, n), lambda i, j: (i, 0)),
                   pl.BlockSpec((bm, n), lambda i, j: (i, 0))],
        out_shape=[jax.ShapeDtypeStruct((m, n), _f32),
                   jax.ShapeDtypeStruct((m, n), _bf16)],
        compiler_params=_params(("arbitrary", "arbitrary")),
    )(x, w, res, g.reshape(1, n), b.reshape(1, n))


def _sgu_kernel(u_ref, v_ref, vg_ref, vb_ref, w_ref, bs_ref, o_ref, *, nchunk, ngroup):
    vn = _layer_norm(v_ref[...], vg_ref[...], vb_ref[...]).astype(_bf16)
    row = lax.broadcasted_iota(jnp.int32, (CHUNK, CHUNK), 0)
    col = lax.broadcasted_iota(jnp.int32, (CHUNK, CHUNK), 1)
    causal = col <= row
    for g in range(ngroup):
        lanes = slice(g * CHUNK, (g + 1) * CHUNK)
        wg = jnp.where(causal, w_ref[g], 0.0).astype(_bf16)
        rhs = jnp.concatenate(
            [vn[c * CHUNK:(c + 1) * CHUNK, lanes] for c in range(nchunk)], axis=1)
        vm = jnp.dot(wg, rhs, preferred_element_type=_f32) + bs_ref[g]
        for c in range(nchunk):
            rows = slice(c * CHUNK, (c + 1) * CHUNK)
            o_ref[rows, lanes] = (u_ref[rows, lanes] * vm[:, c * CHUNK:(c + 1) * CHUNK]).astype(_bf16)


def _sgu(z, vg, vb, w_s, b_s, bm):
    m, n2 = z.shape
    d = n2 // 2
    ngroup = d // CHUNK
    return pl.pallas_call(
        functools.partial(_sgu_kernel, nchunk=bm // CHUNK, ngroup=ngroup),
        grid=(m // bm,),
        in_specs=[pl.BlockSpec((bm, d), lambda i: (i, 0)),
                  pl.BlockSpec((bm, d), lambda i: (i, 1)),
                  pl.BlockSpec((1, d), lambda i: (0, 0)),
                  pl.BlockSpec((1, d), lambda i: (0, 0)),
                  pl.BlockSpec((ngroup, CHUNK, CHUNK), lambda i: (0, 0, 0)),
                  pl.BlockSpec((ngroup, CHUNK, 1), lambda i: (0, 0, 0))],
        out_specs=pl.BlockSpec((bm, d), lambda i: (i, 0)),
        out_shape=jax.ShapeDtypeStruct((m, d), _bf16),
        compiler_params=_params(("arbitrary",)),
    )(z, z, vg.reshape(1, d), vb.reshape(1, d), w_s, b_s.reshape(ngroup, CHUNK, 1))


def _attn_kernel(sink_ref, q_ref, kp_ref, kc_ref, vp_ref, vc_ref, o_ref, *, gqa):
    n = pl.program_id(1)
    blk = q_ref.shape[0]
    npair = gqa // 2
    lane = lax.broadcasted_iota(jnp.int32, (2 * blk, 2 * HEAD_DIM), 1)
    lower = lane < HEAD_DIM
    qi = lax.broadcasted_iota(jnp.int32, (blk, 2 * blk), 0)
    kj = lax.broadcasted_iota(jnp.int32, (blk, 2 * blk), 1)
    valid = (kj > qi) & (kj <= qi + WINDOW) & ((n > 0) | (kj >= blk))
    valid = jnp.concatenate([valid] * npair, axis=0)

    def halves(prev_ref, cur_ref, hp):
        lanes = slice(hp * 2 * HEAD_DIM, (hp + 1) * 2 * HEAD_DIM)
        both = jnp.concatenate([prev_ref[:, lanes], cur_ref[:, lanes]], axis=0).astype(_f32)
        swapped = pltpu.roll(both, HEAD_DIM, 1)
        zero = jnp.zeros_like(both)
        head_a = (jnp.where(lower, both, zero), jnp.where(lower, zero, swapped))
        head_b = (jnp.where(lower, swapped, zero), jnp.where(lower, zero, both))
        return [tuple(t.astype(_bf16) for t in head_a), tuple(t.astype(_bf16) for t in head_b)]

    for hp in range(N_KV // 2):
        k_heads = halves(kp_ref, kc_ref, hp)
        v_heads = halves(vp_ref, vc_ref, hp)
        for sub in range(2):
            h = 2 * hp + sub
            base = h * gqa * HEAD_DIM
            qs = jnp.concatenate(
                [q_ref[:, base + p * 2 * HEAD_DIM: base + (p + 1) * 2 * HEAD_DIM] for p in range(npair)],
                axis=0)
            out = None
            for slot in range(2):
                s = lax.dot_general(qs, k_heads[sub][slot], (((1,), (1,)), ((), ())),
                                    preferred_element_type=_f32)
                s = jnp.where(valid, s, -jnp.inf)
                sink = jnp.concatenate(
                    [jnp.full((blk, 1), sink_ref[h * gqa + 2 * p + slot], _f32) for p in range(npair)],
                    axis=0)
                mx = jnp.maximum(jnp.max(s, axis=-1, keepdims=True), sink)
                p_un = jnp.exp(s - mx)
                denom = jnp.sum(p_un, axis=-1, keepdims=True) + jnp.exp(sink - mx)
                prob = (p_un / denom).astype(_bf16)
                part = jnp.dot(prob, v_heads[sub][slot], preferred_element_type=_f32)
                out = part if out is None else out + part
            for p in range(npair):
                o_ref[:, base + p * 2 * HEAD_DIM: base + (p + 1) * 2 * HEAD_DIM] = (
                    out[p * blk:(p + 1) * blk].astype(o_ref.dtype))


def _attention(q, kv, sinks, batch, seq):
    m, d = q.shape
    blk = WINDOW
    nb = seq // blk
    kvw = N_KV * HEAD_DIM
    gqa = d // (N_KV * HEAD_DIM)

    def cur(col):
        return lambda b, n: (b * nb + n, col)

    def prev(col):
        return lambda b, n: (b * nb + jnp.maximum(n - 1, 0), col)

    return pl.pallas_call(
        functools.partial(_attn_kernel, gqa=gqa),
        grid=(batch, nb),
        in_specs=[pl.BlockSpec(memory_space=pltpu.SMEM),
                  pl.BlockSpec((blk, d), cur(0)),
                  pl.BlockSpec((blk, kvw), prev(0)),
                  pl.BlockSpec((blk, kvw), cur(0)),
                  pl.BlockSpec((blk, kvw), prev(1)),
                  pl.BlockSpec((blk, kvw), cur(1))],
        out_specs=pl.BlockSpec((blk, d), cur(0)),
        out_shape=jax.ShapeDtypeStruct((m, d), _bf16),
        compiler_params=_params(("arbitrary", "arbitrary")),
    )(sinks, q, kv, kv, kv, kv)


def _ffn_kernel(hb_ref, hf_ref, wg_ref, wu_ref, cwg_ref, cwu_ref, cbg_ref, cbu_ref, wd_ref,
                g_ref, b_ref, o_ref, ob_ref, tail_ref, *, alpha, nres, bres, blocks_per_seq):
    i = pl.program_id(0)
    f = pl.program_id(1)
    nf = pl.num_programs(1)
    bm = hb_ref.shape[0]
    seq_start = (i % blocks_per_seq) == 0
    row8 = lax.broadcasted_iota(jnp.int32, (SUBLANES, wg_ref.shape[1]), 0)

    def conv(z, half, cw_ref, cb_ref):
        w0, w1, w2 = cw_ref[0:1, :], cw_ref[1:2, :], cw_ref[2:3, :]
        bias = cb_ref[...]
        tail = jnp.where(seq_start, 0.0, tail_ref[f, half])
        tail_ref[f, half] = z[bm - SUBLANES:, :]
        body = w2 * z + w1 * pltpu.roll(z, 1, 0) + w0 * pltpu.roll(z, 2, 0) + bias
        zh = z[:SUBLANES, :]
        zh1 = jnp.where(row8 < 1, pltpu.roll(tail, 1, 0), pltpu.roll(zh, 1, 0))
        zh2 = jnp.where(row8 < 2, pltpu.roll(tail, 2, 0), pltpu.roll(zh, 2, 0))
        head = w2 * zh + w1 * zh1 + w0 * zh2 + bias
        return jnp.concatenate([head, body[SUBLANES:, :]], axis=0)

    hb = hb_ref[...]
    gate = conv(jnp.dot(hb, wg_ref[...], preferred_element_type=_f32), 0, cwg_ref, cbg_ref)
    up = conv(jnp.dot(hb, wu_ref[...], preferred_element_type=_f32), 1, cwu_ref, cbu_ref)
    act = (gate * jax.nn.sigmoid(gate) * up).astype(_bf16)

    @pl.when(f == 0)
    def _():
        o_ref[...] = jnp.zeros_like(o_ref)

    o_ref[...] += jnp.dot(act, wd_ref[...], preferred_element_type=_f32)

    for s in range(nres):
        @pl.when(f == s)
        def _():
            o_ref[:, s * bres:(s + 1) * bres] += alpha * hf_ref[...]

    @pl.when(f == nf - 1)
    def _():
        h = _layer_norm(o_ref[...], g_ref[...], b_ref[...])
        o_ref[...] = h
        ob_ref[...] = h.astype(_bf16)


def _ffn(hb, hf, w_up, conv_w, conv_b, w_down, g, b, alpha, seq, bm, bf, bres):
    m, d = hb.shape
    d_ff = w_down.shape[0]
    nf = d_ff // bf
    nres = d // bres
    assert nres <= nf and seq % bm == 0
    conv_b2 = conv_b.reshape(1, 2 * d_ff)
    return pl.pallas_call(
        functools.partial(_ffn_kernel, alpha=alpha, nres=nres, bres=bres, blocks_per_seq=seq // bm),
        grid=(m // bm, nf),
        in_specs=[pl.BlockSpec((bm, d), lambda i, f: (i, 0)),
                  pl.BlockSpec((bm, bres), lambda i, f: (i, jnp.minimum(f, nres - 1))),
                  pl.BlockSpec((d, bf), lambda i, f: (0, f)),
                  pl.BlockSpec((d, bf), lambda i, f: (0, nf + f)),
                  pl.BlockSpec((CONV_W, bf), lambda i, f: (0, f)),
                  pl.BlockSpec((CONV_W, bf), lambda i, f: (0, nf + f)),
                  pl.BlockSpec((1, bf), lambda i, f: (0, f)),
                  pl.BlockSpec((1, bf), lambda i, f: (0, nf + f)),
                  pl.BlockSpec((bf, d), lambda i, f: (f, 0)),
                  pl.BlockSpec((1, d), lambda i, f: (0, 0)),
                  pl.BlockSpec((1, d), lambda i, f: (0, 0))],
        out_specs=[pl.BlockSpec((bm, d), lambda i, f: (i, 0)),
                   pl.BlockSpec((bm, d), lambda i, f: (i, 0))],
        out_shape=[jax.ShapeDtypeStruct((m, d), _f32),
                   jax.ShapeDtypeStruct((m, d), _bf16)],
        scratch_shapes=[pltpu.VMEM((nf, 2, SUBLANES, bf), _f32)],
        compiler_params=_params(("arbitrary", "arbitrary")),
    )(hb, hf, w_up, w_up, conv_w, conv_w, conv_b2, conv_b2, w_down, g.reshape(1, d), b.reshape(1, d))


def _gelu(z):
    return 0.5 * z * (1.0 + lax.erf(z * (2.0 ** -0.5)))


def _identity(z):
    return z


def kernel(x, mix_in_a, norm_v_a_g, norm_v_a_b, sgu_w, sgu_b, mix_out_a, w_kv, mix_in_b, sinks,
           mix_out_b, ffn_up, ffn_conv_w, ffn_conv_b, ffn_down, ln_g, ln_b):
    batch, seq, d = x.shape
    depth = ffn_up.shape[0]
    n_a = mix_in_a.shape[0]
    alpha = (2.0 * depth) ** 0.25
    scale = HEAD_DIM ** -0.5
    m = batch * seq

    hf = x.reshape(m, d)
    hb = hf.astype(_bf16)
    kv = None
    for l in range(depth):
        if l < n_a:
            z = _mm_act(hb, mix_in_a[l].astype(_bf16), _gelu, _f32, bm=1024, bn=512)
            gated = _sgu(z, norm_v_a_g[l], norm_v_a_b[l], sgu_w[l], sgu_b[l], bm=256)
            mix_in, w_out = gated, mix_out_a[l]
        else:
            j = l - n_a
            if kv is None:
                kv = _mm_act(hb, w_kv.astype(_bf16), _identity, _bf16, bm=1024, bn=512)
            q = _mm_act(hb, mix_in_b[j].astype(_bf16), lambda t: t * scale, _bf16, bm=1024, bn=512)
            mix_in, w_out = _attention(q, kv, sinks[j], batch, seq), mix_out_b[j]
        hf, hb = _mm_res_ln(mix_in, w_out.astype(_bf16), hf, ln_g[l, 0], ln_b[l, 0], alpha,
                            bm=512, bn=512)
        hf, hb = _ffn(hb, hf, ffn_up[l].astype(_bf16), ffn_conv_w[l], ffn_conv_b[l],
                      ffn_down[l].astype(_bf16), ln_g[l, 1], ln_b[l, 1], alpha, seq,
                      bm=512, bf=256, bres=512)
    return hf.reshape(batch, seq, d)
```

```python
import functools

import jax
import jax.numpy as jnp
from jax import lax
from jax.experimental import pallas as pl
from jax.experimental.pallas import tpu as pltpu

LN_EPS = 1e-5
CHUNK = 128
HEAD_DIM = 64
N_KV = 8
WINDOW = 128
CONV_W = 3

VMEM_LIMIT_BYTES = 56 * 1024 * 1024
SUBLANES = 8
LN_ROWS = 16
LN_UNROLL = 8
CONV_TILE_ROWS = 32

_f32 = jnp.float32
_bf16 = jnp.bfloat16


def _params(sem):
    return pltpu.CompilerParams(dimension_semantics=sem, vmem_limit_bytes=VMEM_LIMIT_BYTES)


def _layer_norm(y, g, b):
    mu = jnp.mean(y, axis=-1, keepdims=True)
    yc = y - mu
    var = jnp.mean(yc * yc, axis=-1, keepdims=True)
    return yc * lax.rsqrt(var + LN_EPS) * g + b


def _layer_norm_rows(o_ref, ob_ref, g_ref, b_ref, stat_ref):
    ngroups = o_ref.shape[0] // LN_ROWS

    def group(r):
        return pl.ds(pl.multiple_of(r * LN_ROWS, LN_ROWS), LN_ROWS)

    def mean_pass(r, carry):
        rows = group(r)
        stat_ref[0, rows, :] = jnp.mean(o_ref[rows, :], axis=-1, keepdims=True)
        return carry

    def var_pass(r, carry):
        rows = group(r)
        yc = o_ref[rows, :] - stat_ref[0, rows, :]
        stat_ref[1, rows, :] = lax.rsqrt(jnp.mean(yc * yc, axis=-1, keepdims=True) + LN_EPS)
        return carry

    def norm_pass(r, carry):
        rows = group(r)
        h = (o_ref[rows, :] - stat_ref[0, rows, :]) * stat_ref[1, rows, :] * g_ref[...] + b_ref[...]
        o_ref[rows, :] = h
        ob_ref[rows, :] = h.astype(_bf16)
        return carry

    for one_pass in (mean_pass, var_pass, norm_pass):
        lax.fori_loop(0, ngroups, one_pass, 0, unroll=LN_UNROLL)


def _mm_act_kernel(x_ref, w_ref, o_ref, *, act):
    z = jnp.dot(x_ref[...], w_ref[...], preferred_element_type=_f32)
    o_ref[...] = act(z).astype(o_ref.dtype)


def _mm_act(x, w, act, out_dtype, bm, bn):
    m, k = x.shape
    n = w.shape[1]
    return pl.pallas_call(
        functools.partial(_mm_act_kernel, act=act),
        grid=(m // bm, n // bn),
        in_specs=[pl.BlockSpec((bm, k), lambda i, j: (i, 0)),
                  pl.BlockSpec((k, bn), lambda i, j: (0, j))],
        out_specs=pl.BlockSpec((bm, bn), lambda i, j: (i, j)),
        out_shape=jax.ShapeDtypeStruct((m, n), out_dtype),
        compiler_params=_params(("arbitrary", "arbitrary")),
    )(x, w)


def _mm_res_ln_kernel(x_ref, w_ref, res_ref, g_ref, b_ref, o_ref, ob_ref, stat_ref, *, alpha, bn, nslab):
    j = pl.program_id(1)
    y = alpha * res_ref[...] + jnp.dot(x_ref[...], w_ref[j], preferred_element_type=_f32)
    for s in range(nslab):
        @pl.when(j == s)
        def _():
            o_ref[:, s * bn:(s + 1) * bn] = y

    @pl.when(j == nslab - 1)
    def _():
        _layer_norm_rows(o_ref, ob_ref, g_ref, b_ref, stat_ref)


def _mm_res_ln(x, w_slabs, res, g, b, alpha, bm):
    m, k = x.shape
    nslab, _, bn = w_slabs.shape
    n = nslab * bn
    return pl.pallas_call(
        functools.partial(_mm_res_ln_kernel, alpha=alpha, bn=bn, nslab=nslab),
        grid=(m // bm, nslab),
        in_specs=[pl.BlockSpec((bm, k), lambda i, j: (i, 0)),
                  pl.BlockSpec((nslab, k, bn), lambda i, j: (0, 0, 0), pipeline_mode=pl.Buffered(1)),
                  pl.BlockSpec((bm, bn), lambda i, j: (i, j)),
                  pl.BlockSpec((1, n), lambda i, j: (0, 0)),
                  pl.BlockSpec((1, n), lambda i, j: (0, 0))],
        out_specs=[pl.BlockSpec((bm, n), lambda i, j: (i, 0)),
                   pl.BlockSpec((bm, n), lambda i, j: (i, 0))],
        out_shape=[jax.ShapeDtypeStruct((m, n), _f32),
                   jax.ShapeDtypeStruct((m, n), _bf16)],
        scratch_shapes=[pltpu.VMEM((2, bm, 1), _f32)],
        compiler_params=_params(("arbitrary", "arbitrary")),
    )(x, w_slabs, res, g.reshape(1, n), b.reshape(1, n))


def _column_slabs(w, bn):
    k, n = w.shape
    return w.reshape(k, n // bn, bn).transpose(1, 0, 2)


def _sgu_kernel(u_ref, v_ref, vg_ref, vb_ref, w_ref, bs_ref, o_ref, *, nchunk, ngroup):
    vn = _layer_norm(v_ref[...], vg_ref[...], vb_ref[...]).astype(_bf16)
    row = lax.broadcasted_iota(jnp.int32, (CHUNK, CHUNK), 0)
    col = lax.broadcasted_iota(jnp.int32, (CHUNK, CHUNK), 1)
    causal = col <= row
    for g in range(ngroup):
        lanes = slice(g * CHUNK, (g + 1) * CHUNK)
        wg = jnp.where(causal, w_ref[g], 0.0).astype(_bf16)
        rhs = jnp.concatenate(
            [vn[c * CHUNK:(c + 1) * CHUNK, lanes] for c in range(nchunk)], axis=1)
        vm = jnp.dot(wg, rhs, preferred_element_type=_f32) + bs_ref[g]
        for c in range(nchunk):
            rows = slice(c * CHUNK, (c + 1) * CHUNK)
            o_ref[rows, lanes] = (u_ref[rows, lanes] * vm[:, c * CHUNK:(c + 1) * CHUNK]).astype(_bf16)


def _sgu(z, vg, vb, w_s, b_s, bm):
    m, n2 = z.shape
    d = n2 // 2
    ngroup = d // CHUNK
    return pl.pallas_call(
        functools.partial(_sgu_kernel, nchunk=bm // CHUNK, ngroup=ngroup),
        grid=(m // bm,),
        in_specs=[pl.BlockSpec((bm, d), lambda i: (i, 0)),
                  pl.BlockSpec((bm, d), lambda i: (i, 1)),
                  pl.BlockSpec((1, d), lambda i: (0, 0)),
                  pl.BlockSpec((1, d), lambda i: (0, 0)),
                  pl.BlockSpec((ngroup, CHUNK, CHUNK), lambda i: (0, 0, 0)),
                  pl.BlockSpec((ngroup, CHUNK, 1), lambda i: (0, 0, 0))],
        out_specs=pl.BlockSpec((bm, d), lambda i: (i, 0)),
        out_shape=jax.ShapeDtypeStruct((m, d), _bf16),
        compiler_params=_params(("arbitrary",)),
    )(z, z, vg.reshape(1, d), vb.reshape(1, d), w_s, b_s.reshape(ngroup, CHUNK, 1))


def _attn_kernel(sink_ref, q_ref, kp_ref, kc_ref, vp_ref, vc_ref, o_ref, *, gqa):
    n = pl.program_id(1)
    blk = q_ref.shape[0]
    npair = gqa // 2
    lane = lax.broadcasted_iota(jnp.int32, (2 * blk, 2 * HEAD_DIM), 1)
    lower = lane < HEAD_DIM
    qi = lax.broadcasted_iota(jnp.int32, (blk, 2 * blk), 0)
    kj = lax.broadcasted_iota(jnp.int32, (blk, 2 * blk), 1)
    valid = (kj > qi) & (kj <= qi + WINDOW) & ((n > 0) | (kj >= blk))
    valid = jnp.concatenate([valid] * npair, axis=0)

    def halves(prev_ref, cur_ref, hp):
        lanes = slice(hp * 2 * HEAD_DIM, (hp + 1) * 2 * HEAD_DIM)
        both = jnp.concatenate([prev_ref[:, lanes], cur_ref[:, lanes]], axis=0).astype(_f32)
        swapped = pltpu.roll(both, HEAD_DIM, 1)
        zero = jnp.zeros_like(both)
        head_a = (jnp.where(lower, both, zero), jnp.where(lower, zero, swapped))
        head_b = (jnp.where(lower, swapped, zero), jnp.where(lower, zero, both))
        return [tuple(t.astype(_bf16) for t in head_a), tuple(t.astype(_bf16) for t in head_b)]

    for hp in range(N_KV // 2):
        k_heads = halves(kp_ref, kc_ref, hp)
        v_heads = halves(vp_ref, vc_ref, hp)
        for sub in range(2):
            h = 2 * hp + sub
            base = h * gqa * HEAD_DIM
            qs = jnp.concatenate(
                [q_ref[:, base + p * 2 * HEAD_DIM: base + (p + 1) * 2 * HEAD_DIM] for p in range(npair)],
                axis=0)
            out = None
            for slot in range(2):
                s = lax.dot_general(qs, k_heads[sub][slot], (((1,), (1,)), ((), ())),
                                    preferred_element_type=_f32)
                s = jnp.where(valid, s, -jnp.inf)
                sink = jnp.concatenate(
                    [jnp.full((blk, 1), sink_ref[h * gqa + 2 * p + slot], _f32) for p in range(npair)],
                    axis=0)
                mx = jnp.maximum(jnp.max(s, axis=-1, keepdims=True), sink)
                p_un = jnp.exp(s - mx)
                denom = jnp.sum(p_un, axis=-1, keepdims=True) + jnp.exp(sink - mx)
                prob = (p_un / denom).astype(_bf16)
                part = jnp.dot(prob, v_heads[sub][slot], preferred_element_type=_f32)
                out = part if out is None else out + part
            for p in range(npair):
                o_ref[:, base + p * 2 * HEAD_DIM: base + (p + 1) * 2 * HEAD_DIM] = (
                    out[p * blk:(p + 1) * blk].astype(o_ref.dtype))


def _attention(q, kv, sinks, batch, seq):
    m, d = q.shape
    blk = WINDOW
    nb = seq // blk
    kvw = N_KV * HEAD_DIM
    gqa = d // (N_KV * HEAD_DIM)

    def cur(col):
        return lambda b, n: (b * nb + n, col)

    def prev(col):
        return lambda b, n: (b * nb + jnp.maximum(n - 1, 0), col)

    return pl.pallas_call(
        functools.partial(_attn_kernel, gqa=gqa),
        grid=(batch, nb),
        in_specs=[pl.BlockSpec(memory_space=pltpu.SMEM),
                  pl.BlockSpec((blk, d), cur(0)),
                  pl.BlockSpec((blk, kvw), prev(0)),
                  pl.BlockSpec((blk, kvw), cur(0)),
                  pl.BlockSpec((blk, kvw), prev(1)),
                  pl.BlockSpec((blk, kvw), cur(1))],
        out_specs=pl.BlockSpec((blk, d), cur(0)),
        out_shape=jax.ShapeDtypeStruct((m, d), _bf16),
        compiler_params=_params(("arbitrary", "arbitrary")),
    )(sinks, q, kv, kv, kv, kv)


def _ffn_kernel(hb_ref, hf_ref, wg_ref, wu_ref, cwg_ref, cwu_ref, cbg_ref, cbu_ref, wd_ref,
                g_ref, b_ref, o_ref, ob_ref, zs0_ref, zs1_ref, act0_ref, act1_ref, tail_ref, stat_ref,
                *, alpha, nf, nres, bres, blocks_per_seq, nchunks):
    t = pl.program_id(0)
    bm = hb_ref.shape[0]
    ta = jnp.minimum(t, nchunks - 1)
    fa = ta % nf
    seq_start = ((ta // nf) % blocks_per_seq) == 0
    fc = jnp.maximum(t - 2, 0) % nf

    @pl.when(t == 0)
    def _():
        zs0_ref[...] = jnp.zeros_like(zs0_ref)
        zs1_ref[...] = jnp.zeros_like(zs1_ref)
        act0_ref[...] = jnp.zeros_like(act0_ref)
        act1_ref[...] = jnp.zeros_like(act1_ref)

    @pl.when(fc == 0)
    def _():
        o_ref[...] = jnp.zeros_like(o_ref)

    def conv_tile(zs_ref, half, cw_ref, cb_ref, r):
        rows = CONV_TILE_ROWS
        z0 = zs_ref[half, SUBLANES + r:SUBLANES + r + rows, :]
        z1 = zs_ref[half, SUBLANES - 1 + r:SUBLANES - 1 + r + rows, :]
        z2 = zs_ref[half, SUBLANES - 2 + r:SUBLANES - 2 + r + rows, :]
        return cw_ref[0:1, :] * z2 + cw_ref[1:2, :] * z1 + cw_ref[2:3, :] * z0 + cb_ref[...]

    def stages(zs_a, zs_b, act_b, act_c):
        hb = hb_ref[...]
        for half, w_ref in ((0, wg_ref), (1, wu_ref)):
            zs_a[half, :SUBLANES, :] = jnp.where(seq_start, 0.0, tail_ref[fa, half])
            zs_a[half, SUBLANES:, :] = jnp.dot(hb, w_ref[...], preferred_element_type=_f32)
            tail_ref[fa, half] = zs_a[half, bm:, :]
        ntile = bm // CONV_TILE_ROWS
        bn = o_ref.shape[1] // ntile
        for k in range(ntile):
            cols = slice(k * bn, (k + 1) * bn)
            o_ref[:, cols] += jnp.dot(act_c[...], wd_ref[:, cols], preferred_element_type=_f32)
            r = k * CONV_TILE_ROWS
            gate = conv_tile(zs_b, 0, cwg_ref, cbg_ref, r)
            up = conv_tile(zs_b, 1, cwu_ref, cbu_ref, r)
            act_b[r:r + CONV_TILE_ROWS, :] = (gate * jax.nn.sigmoid(gate) * up).astype(_bf16)

    @pl.when(t % 2 == 0)
    def _():
        stages(zs0_ref, zs1_ref, act1_ref, act0_ref)

    @pl.when(t % 2 == 1)
    def _():
        stages(zs1_ref, zs0_ref, act0_ref, act1_ref)

    for s in range(nres):
        @pl.when((fc == s) & (t >= 2))
        def _():
            o_ref[:, s * bres:(s + 1) * bres] += alpha * hf_ref[...]

    @pl.when((fc == nf - 1) & (t >= 2))
    def _():
        _layer_norm_rows(o_ref, ob_ref, g_ref, b_ref, stat_ref)


def _ffn(hb, hf, w_up, conv_w, conv_b, w_down, g, b, alpha, seq, bm, bf, bres):
    m, d = hb.shape
    d_ff = w_down.shape[0]
    nf = d_ff // bf
    nres = d // bres
    nchunks = (m // bm) * nf
    assert nres <= nf and seq % bm == 0 and bm % CONV_TILE_ROWS == 0
    conv_b2 = conv_b.reshape(1, 2 * d_ff)

    def chunk_a(t):
        return jnp.minimum(t, nchunks - 1)

    def chunk_b(t):
        return jnp.clip(t - 1, 0, nchunks - 1)

    def chunk_c(t):
        return jnp.maximum(t - 2, 0)

    return pl.pallas_call(
        functools.partial(_ffn_kernel, alpha=alpha, nf=nf, nres=nres, bres=bres,
                          blocks_per_seq=seq // bm, nchunks=nchunks),
        grid=(nchunks + 2,),
        in_specs=[pl.BlockSpec((bm, d), lambda t: (chunk_a(t) // nf, 0)),
                  pl.BlockSpec((bm, bres),
                               lambda t: (chunk_c(t) // nf, jnp.minimum(chunk_c(t) % nf, nres - 1))),
                  pl.BlockSpec((d, bf), lambda t: (0, chunk_a(t) % nf)),
                  pl.BlockSpec((d, bf), lambda t: (0, nf + chunk_a(t) % nf)),
                  pl.BlockSpec((CONV_W, bf), lambda t: (0, chunk_b(t) % nf)),
                  pl.BlockSpec((CONV_W, bf), lambda t: (0, nf + chunk_b(t) % nf)),
                  pl.BlockSpec((1, bf), lambda t: (0, chunk_b(t) % nf)),
                  pl.BlockSpec((1, bf), lambda t: (0, nf + chunk_b(t) % nf)),
                  pl.BlockSpec((bf, d), lambda t: (chunk_c(t) % nf, 0)),
                  pl.BlockSpec((1, d), lambda t: (0, 0)),
                  pl.BlockSpec((1, d), lambda t: (0, 0))],
        out_specs=[pl.BlockSpec((bm, d), lambda t: (chunk_c(t) // nf, 0)),
                   pl.BlockSpec((bm, d), lambda t: (chunk_c(t) // nf, 0))],
        out_shape=[jax.ShapeDtypeStruct((m, d), _f32),
                   jax.ShapeDtypeStruct((m, d), _bf16)],
        scratch_shapes=[pltpu.VMEM((2, SUBLANES + bm, bf), _f32),
                        pltpu.VMEM((2, SUBLANES + bm, bf), _f32),
                        pltpu.VMEM((bm, bf), _bf16),
                        pltpu.VMEM((bm, bf), _bf16),
                        pltpu.VMEM((nf, 2, SUBLANES, bf), _f32),
                        pltpu.VMEM((2, bm, 1), _f32)],
        compiler_params=_params(("arbitrary",)),
    )(hb, hf, w_up, w_up, conv_w, conv_w, conv_b2, conv_b2, w_down, g.reshape(1, d), b.reshape(1, d))


def _gelu(z):
    return 0.5 * z * (1.0 + lax.erf(z * (2.0 ** -0.5)))


def _identity(z):
    return z


def kernel(x, mix_in_a, norm_v_a_g, norm_v_a_b, sgu_w, sgu_b, mix_out_a, w_kv, mix_in_b, sinks,
           mix_out_b, ffn_up, ffn_conv_w, ffn_conv_b, ffn_down, ln_g, ln_b):
    batch, seq, d = x.shape
    depth = ffn_up.shape[0]
    n_a = mix_in_a.shape[0]
    alpha = (2.0 * depth) ** 0.25
    scale = HEAD_DIM ** -0.5
    m = batch * seq

    hf = x.reshape(m, d)
    hb = hf.astype(_bf16)
    kv = None
    for l in range(depth):
        if l < n_a:
            z = _mm_act(hb, mix_in_a[l].astype(_bf16), _gelu, _f32, bm=1024, bn=512)
            gated = _sgu(z, norm_v_a_g[l], norm_v_a_b[l], sgu_w[l], sgu_b[l], bm=256)
            mix_in, w_out = gated, mix_out_a[l]
        else:
            j = l - n_a
            if kv is None:
                kv = _mm_act(hb, w_kv.astype(_bf16), _identity, _bf16, bm=1024, bn=512)
            q = _mm_act(hb, mix_in_b[j].astype(_bf16), lambda t: t * scale, _bf16, bm=1024, bn=512)
            mix_in, w_out = _attention(q, kv, sinks[j], batch, seq), mix_out_b[j]
        hf, hb = _mm_res_ln(mix_in, _column_slabs(w_out.astype(_bf16), 1024), hf,
                            ln_g[l, 0], ln_b[l, 0], alpha, bm=256)
        hf, hb = _ffn(hb, hf, ffn_up[l].astype(_bf16), ffn_conv_w[l], ffn_conv_b[l],
                      ffn_down[l].astype(_bf16), ln_g[l, 1], ln_b[l, 1], alpha, seq,
                      bm=512, bf=256, bres=512)
    return hf.reshape(batch, seq, d)
```

```python
import functools

import jax
import jax.numpy as jnp
from jax import lax
from jax.experimental import pallas as pl
from jax.experimental.pallas import tpu as pltpu

LN_EPS = 1e-5
CHUNK = 128
HEAD_DIM = 64
N_KV = 8
WINDOW = 128
CONV_W = 3

VMEM_LIMIT_BYTES = 56 * 1024 * 1024
SUBLANES = 8
LN_ROWS = 16
LN_UNROLL = 8
CONV_TILE_ROWS = 32

_f32 = jnp.float32
_bf16 = jnp.bfloat16


def _params(sem):
    return pltpu.CompilerParams(dimension_semantics=sem, vmem_limit_bytes=VMEM_LIMIT_BYTES)


def _layer_norm(y, g, b):
    mu = jnp.mean(y, axis=-1, keepdims=True)
    yc = y - mu
    var = jnp.mean(yc * yc, axis=-1, keepdims=True)
    return yc * lax.rsqrt(var + LN_EPS) * g + b


def _layer_norm_rows(o_ref, ob_ref, g_ref, b_ref, stat_ref):
    ngroups = o_ref.shape[0] // LN_ROWS

    def group(r):
        return pl.ds(pl.multiple_of(r * LN_ROWS, LN_ROWS), LN_ROWS)

    def mean_pass(r, carry):
        rows = group(r)
        stat_ref[0, rows, :] = jnp.mean(o_ref[rows, :], axis=-1, keepdims=True)
        return carry

    def var_pass(r, carry):
        rows = group(r)
        yc = o_ref[rows, :] - stat_ref[0, rows, :]
        stat_ref[1, rows, :] = lax.rsqrt(jnp.mean(yc * yc, axis=-1, keepdims=True) + LN_EPS)
        return carry

    def norm_pass(r, carry):
        rows = group(r)
        h = (o_ref[rows, :] - stat_ref[0, rows, :]) * stat_ref[1, rows, :] * g_ref[...] + b_ref[...]
        o_ref[rows, :] = h
        ob_ref[rows, :] = h.astype(_bf16)
        return carry

    for one_pass in (mean_pass, var_pass, norm_pass):
        lax.fori_loop(0, ngroups, one_pass, 0, unroll=LN_UNROLL)


def _mm_act_kernel(x_ref, w_ref, o_ref, *, act):
    z = jnp.dot(x_ref[...], w_ref[...], preferred_element_type=_f32)
    o_ref[...] = act(z).astype(o_ref.dtype)


def _mm_act(x, w, layer, act, out_dtype, bm, bn):
    m, k = x.shape
    n = w.shape[2]
    return pl.pallas_call(
        functools.partial(_mm_act_kernel, act=act),
        grid=(m // bm, n // bn),
        in_specs=[pl.BlockSpec((bm, k), lambda i, j: (i, 0)),
                  pl.BlockSpec((None, k, bn), lambda i, j: (layer, 0, j))],
        out_specs=pl.BlockSpec((bm, bn), lambda i, j: (i, j)),
        out_shape=jax.ShapeDtypeStruct((m, n), out_dtype),
        compiler_params=_params(("arbitrary", "arbitrary")),
    )(x, w)


def _cast_kernel(x_ref, o_ref):
    o_ref[...] = x_ref[...].astype(o_ref.dtype)


def _cast_bf16(w, block_rows):
    c = w.shape[-1]
    w2 = w.reshape(-1, c)
    r = w2.shape[0]
    assert r % block_rows == 0
    out = pl.pallas_call(
        _cast_kernel,
        grid=(r // block_rows,),
        in_specs=[pl.BlockSpec((block_rows, c), lambda i: (i, 0))],
        out_specs=pl.BlockSpec((block_rows, c), lambda i: (i, 0)),
        out_shape=jax.ShapeDtypeStruct((r, c), _bf16),
        compiler_params=_params(("arbitrary",)),
    )(w2)
    return out.reshape(w.shape)


def _cast_bf16_column_slabs(w, layer, bn, block_rows):
    _, k, n = w.shape
    return pl.pallas_call(
        _cast_kernel,
        grid=(k // block_rows, n // bn),
        in_specs=[pl.BlockSpec((None, block_rows, bn), lambda i, j: (layer, i, j))],
        out_specs=pl.BlockSpec((None, block_rows, bn), lambda i, j: (j, i, 0)),
        out_shape=jax.ShapeDtypeStruct((n // bn, k, bn), _bf16),
        compiler_params=_params(("arbitrary", "arbitrary")),
    )(w)


def _mm_res_ln_kernel(x_ref, w_ref, res_ref, g_ref, b_ref, o_ref, ob_ref, stat_ref, *, alpha, bn, nslab):
    j = pl.program_id(1)
    y = alpha * res_ref[...] + jnp.dot(x_ref[...], w_ref[j], preferred_element_type=_f32)
    for s in range(nslab):
        @pl.when(j == s)
        def _():
            o_ref[:, s * bn:(s + 1) * bn] = y

    @pl.when(j == nslab - 1)
    def _():
        _layer_norm_rows(o_ref, ob_ref, g_ref, b_ref, stat_ref)


def _mm_res_ln(x, w_slabs, res, g, b, alpha, bm):
    m, k = x.shape
    nslab, _, bn = w_slabs.shape
    n = nslab * bn
    return pl.pallas_call(
        functools.partial(_mm_res_ln_kernel, alpha=alpha, bn=bn, nslab=nslab),
        grid=(m // bm, nslab),
        in_specs=[pl.BlockSpec((bm, k), lambda i, j: (i, 0)),
                  pl.BlockSpec((nslab, k, bn), lambda i, j: (0, 0, 0), pipeline_mode=pl.Buffered(1)),
                  pl.BlockSpec((bm, bn), lambda i, j: (i, j)),
                  pl.BlockSpec((1, n), lambda i, j: (0, 0)),
                  pl.BlockSpec((1, n), lambda i, j: (0, 0))],
        out_specs=[pl.BlockSpec((bm, n), lambda i, j: (i, 0)),
                   pl.BlockSpec((bm, n), lambda i, j: (i, 0))],
        out_shape=[jax.ShapeDtypeStruct((m, n), _f32),
                   jax.ShapeDtypeStruct((m, n), _bf16)],
        scratch_shapes=[pltpu.VMEM((2, bm, 1), _f32)],
        compiler_params=_params(("arbitrary", "arbitrary")),
    )(x, w_slabs, res, g.reshape(1, n), b.reshape(1, n))


def _sgu_kernel(u_ref, v_ref, vg_ref, vb_ref, w_ref, bs_ref, o_ref, *, nchunk, ngroup):
    vn = _layer_norm(v_ref[...], vg_ref[...], vb_ref[...]).astype(_bf16)
    row = lax.broadcasted_iota(jnp.int32, (CHUNK, CHUNK), 0)
    col = lax.broadcasted_iota(jnp.int32, (CHUNK, CHUNK), 1)
    causal = col <= row
    for g in range(ngroup):
        lanes = slice(g * CHUNK, (g + 1) * CHUNK)
        wg = jnp.where(causal, w_ref[g], 0.0).astype(_bf16)
        rhs = jnp.concatenate(
            [vn[c * CHUNK:(c + 1) * CHUNK, lanes] for c in range(nchunk)], axis=1)
        vm = jnp.dot(wg, rhs, preferred_element_type=_f32) + bs_ref[g]
        for c in range(nchunk):
            rows = slice(c * CHUNK, (c + 1) * CHUNK)
            o_ref[rows, lanes] = (u_ref[rows, lanes] * vm[:, c * CHUNK:(c + 1) * CHUNK]).astype(_bf16)


def _sgu(z, vg, vb, w_s, b_s, bm):
    m, n2 = z.shape
    d = n2 // 2
    ngroup = d // CHUNK
    return pl.pallas_call(
        functools.partial(_sgu_kernel, nchunk=bm // CHUNK, ngroup=ngroup),
        grid=(m // bm,),
        in_specs=[pl.BlockSpec((bm, d), lambda i: (i, 0)),
                  pl.BlockSpec((bm, d), lambda i: (i, 1)),
                  pl.BlockSpec((1, d), lambda i: (0, 0)),
                  pl.BlockSpec((1, d), lambda i: (0, 0)),
                  pl.BlockSpec((ngroup, CHUNK, CHUNK), lambda i: (0, 0, 0)),
                  pl.BlockSpec((ngroup, CHUNK, 1), lambda i: (0, 0, 0))],
        out_specs=pl.BlockSpec((bm, d), lambda i: (i, 0)),
        out_shape=jax.ShapeDtypeStruct((m, d), _bf16),
        compiler_params=_params(("arbitrary",)),
    )(z, z, vg.reshape(1, d), vb.reshape(1, d), w_s, b_s.reshape(ngroup, CHUNK, 1))


def _attn_kernel(sink_ref, q_ref, kp_ref, kc_ref, vp_ref, vc_ref, o_ref, *, gqa):
    n = pl.program_id(1)
    blk = q_ref.shape[0]
    npair = gqa // 2
    pair_w = 2 * HEAD_DIM
    lower_kv = lax.broadcasted_iota(jnp.int32, (2 * blk, pair_w), 1) < HEAD_DIM
    lower_q = lax.broadcasted_iota(jnp.int32, (blk, pair_w), 1) < HEAD_DIM
    qi = lax.broadcasted_iota(jnp.int32, (blk, 2 * blk), 0)
    kj = lax.broadcasted_iota(jnp.int32, (blk, 2 * blk), 1)
    valid = (kj > qi) & (kj <= qi + WINDOW) & ((n > 0) | (kj >= blk))

    def halves(prev_ref, cur_ref, hp, fill):
        lanes = slice(hp * pair_w, (hp + 1) * pair_w)
        both = jnp.concatenate([prev_ref[:, lanes], cur_ref[:, lanes]], axis=0).astype(_f32)
        swapped = pltpu.roll(both, HEAD_DIM, 1)
        pad = jnp.full_like(both, fill)
        head_a = (jnp.where(lower_kv, both, pad), jnp.where(lower_kv, pad, swapped))
        head_b = (jnp.where(lower_kv, swapped, pad), jnp.where(lower_kv, pad, both))
        return [tuple(t.astype(_bf16) for t in head_a), tuple(t.astype(_bf16) for t in head_b)]

    for hp in range(N_KV // 2):
        k_heads = halves(kp_ref, kc_ref, hp, 0.0)
        v_heads = halves(vp_ref, vc_ref, hp, 1.0)
        for sub in range(2):
            h = 2 * hp + sub
            base = h * gqa * HEAD_DIM
            qs = jnp.concatenate(
                [q_ref[:, base + p * pair_w: base + (p + 1) * pair_w] for p in range(npair)], axis=0)
            pv, sink_terms = [], []
            for slot in range(2):
                s_all = lax.dot_general(qs, k_heads[sub][slot], (((1,), (1,)), ((), ())),
                                        preferred_element_type=_f32)
                e_slabs, terms = [], []
                for p in range(npair):
                    s = jnp.where(valid, s_all[p * blk:(p + 1) * blk], -jnp.inf)
                    sink = sink_ref[h * gqa + 2 * p + slot]
                    mx = jnp.maximum(jnp.max(s, axis=-1, keepdims=True), sink)
                    e_slabs.append(jnp.exp(s - mx).astype(_bf16))
                    terms.append(jnp.exp(sink - mx))
                pv.append(jnp.dot(jnp.concatenate(e_slabs, axis=0), v_heads[sub][slot],
                                  preferred_element_type=_f32))
                sink_terms.append(terms)
            for p in range(npair):
                a = pv[0][p * blk:(p + 1) * blk]
                b = pv[1][p * blk:(p + 1) * blk]
                num = jnp.where(lower_q, a, b)
                den = pltpu.roll(jnp.where(lower_q, b, a), HEAD_DIM, 1)
                den = den + jnp.where(lower_q, sink_terms[0][p], sink_terms[1][p])
                o_ref[:, base + p * pair_w: base + (p + 1) * pair_w] = (num / den).astype(o_ref.dtype)


def _attention(q, kv, sinks, batch, seq):
    m, d = q.shape
    blk = WINDOW
    nb = seq // blk
    kvw = N_KV * HEAD_DIM
    gqa = d // (N_KV * HEAD_DIM)

    def cur(col):
        return lambda b, n: (b * nb + n, col)

    def prev(col):
        return lambda b, n: (b * nb + jnp.maximum(n - 1, 0), col)

    return pl.pallas_call(
        functools.partial(_attn_kernel, gqa=gqa),
        grid=(batch, nb),
        in_specs=[pl.BlockSpec(memory_space=pltpu.SMEM),
                  pl.BlockSpec((blk, d), cur(0)),
                  pl.BlockSpec((blk, kvw), prev(0)),
                  pl.BlockSpec((blk, kvw), cur(0)),
                  pl.BlockSpec((blk, kvw), prev(1)),
                  pl.BlockSpec((blk, kvw), cur(1))],
        out_specs=pl.BlockSpec((blk, d), cur(0)),
        out_shape=jax.ShapeDtypeStruct((m, d), _bf16),
        compiler_params=_params(("arbitrary", "arbitrary")),
    )(sinks, q, kv, kv, kv, kv)


def _ffn_kernel(hb_ref, hf_ref, wg_ref, wu_ref, cwg_ref, cwu_ref, cbg_ref, cbu_ref, wd_ref,
                g_ref, b_ref, o_ref, ob_ref, zs0_ref, zs1_ref, act0_ref, act1_ref, tail_ref, stat_ref,
                *, alpha, nf, nres, bres, blocks_per_seq, nchunks):
    t = pl.program_id(0)
    bm = hb_ref.shape[0]
    ta = jnp.minimum(t, nchunks - 1)
    fa = ta % nf
    seq_start = ((ta // nf) % blocks_per_seq) == 0
    fc = jnp.maximum(t - 2, 0) % nf

    @pl.when(t == 0)
    def _():
        zs0_ref[...] = jnp.zeros_like(zs0_ref)
        zs1_ref[...] = jnp.zeros_like(zs1_ref)
        act0_ref[...] = jnp.zeros_like(act0_ref)
        act1_ref[...] = jnp.zeros_like(act1_ref)

    @pl.when(fc == 0)
    def _():
        o_ref[...] = jnp.zeros_like(o_ref)

    def conv_tile(zs_ref, half, cw_ref, cb_ref, r):
        rows = CONV_TILE_ROWS
        z0 = zs_ref[half, SUBLANES + r:SUBLANES + r + rows, :]
        z1 = zs_ref[half, SUBLANES - 1 + r:SUBLANES - 1 + r + rows, :]
        z2 = zs_ref[half, SUBLANES - 2 + r:SUBLANES - 2 + r + rows, :]
        return cw_ref[0:1, :] * z2 + cw_ref[1:2, :] * z1 + cw_ref[2:3, :] * z0 + cb_ref[...]

    def stages(zs_a, zs_b, act_b, act_c):
        hb = hb_ref[...]
        for half, w_ref in ((0, wg_ref), (1, wu_ref)):
            zs_a[half, :SUBLANES, :] = jnp.where(seq_start, 0.0, tail_ref[fa, half])
            zs_a[half, SUBLANES:, :] = jnp.dot(hb, w_ref[...], preferred_element_type=_f32)
            tail_ref[fa, half] = zs_a[half, bm:, :]
        ntile = bm // CONV_TILE_ROWS
        bn = o_ref.shape[1] // ntile
        for k in range(ntile):
            cols = slice(k * bn, (k + 1) * bn)
            o_ref[:, cols] += jnp.dot(act_c[...], wd_ref[:, cols], preferred_element_type=_f32)
            r = k * CONV_TILE_ROWS
            gate = conv_tile(zs_b, 0, cwg_ref, cbg_ref, r)
            up = conv_tile(zs_b, 1, cwu_ref, cbu_ref, r)
            act_b[r:r + CONV_TILE_ROWS, :] = (gate * jax.nn.sigmoid(gate) * up).astype(_bf16)

    @pl.when(t % 2 == 0)
    def _():
        stages(zs0_ref, zs1_ref, act1_ref, act0_ref)

    @pl.when(t % 2 == 1)
    def _():
        stages(zs1_ref, zs0_ref, act0_ref, act1_ref)

    for s in range(nres):
        @pl.when((fc == s) & (t >= 2))
        def _():
            o_ref[:, s * bres:(s + 1) * bres] += alpha * hf_ref[...]

    @pl.when((fc == nf - 1) & (t >= 2))
    def _():
        _layer_norm_rows(o_ref, ob_ref, g_ref, b_ref, stat_ref)


def _ffn(hb, hf, w_up, w_down, layer, conv_w, conv_b, g, b, alpha, seq, bm, bf, bres):
    m, d = hb.shape
    d_ff = w_down.shape[1]
    nf = d_ff // bf
    nres = d // bres
    nchunks = (m // bm) * nf
    assert nres <= nf and seq % bm == 0 and bm % CONV_TILE_ROWS == 0
    conv_b2 = conv_b.reshape(1, 2 * d_ff)

    def chunk_a(t):
        return jnp.minimum(t, nchunks - 1)

    def chunk_b(t):
        return jnp.clip(t - 1, 0, nchunks - 1)

    def chunk_c(t):
        return jnp.maximum(t - 2, 0)

    return pl.pallas_call(
        functools.partial(_ffn_kernel, alpha=alpha, nf=nf, nres=nres, bres=bres,
                          blocks_per_seq=seq // bm, nchunks=nchunks),
        grid=(nchunks + 2,),
        in_specs=[pl.BlockSpec((bm, d), lambda t: (chunk_a(t) // nf, 0)),
                  pl.BlockSpec((bm, bres),
                               lambda t: (chunk_c(t) // nf, jnp.minimum(chunk_c(t) % nf, nres - 1))),
                  pl.BlockSpec((None, d, bf), lambda t: (layer, 0, chunk_a(t) % nf)),
                  pl.BlockSpec((None, d, bf), lambda t: (layer, 0, nf + chunk_a(t) % nf)),
                  pl.BlockSpec((CONV_W, bf), lambda t: (0, chunk_b(t) % nf)),
                  pl.BlockSpec((CONV_W, bf), lambda t: (0, nf + chunk_b(t) % nf)),
                  pl.BlockSpec((1, bf), lambda t: (0, chunk_b(t) % nf)),
                  pl.BlockSpec((1, bf), lambda t: (0, nf + chunk_b(t) % nf)),
                  pl.BlockSpec((None, bf, d), lambda t: (layer, chunk_c(t) % nf, 0)),
                  pl.BlockSpec((1, d), lambda t: (0, 0)),
                  pl.BlockSpec((1, d), lambda t: (0, 0))],
        out_specs=[pl.BlockSpec((bm, d), lambda t: (chunk_c(t) // nf, 0)),
                   pl.BlockSpec((bm, d), lambda t: (chunk_c(t) // nf, 0))],
        out_shape=[jax.ShapeDtypeStruct((m, d), _f32),
                   jax.ShapeDtypeStruct((m, d), _bf16)],
        scratch_shapes=[pltpu.VMEM((2, SUBLANES + bm, bf), _f32),
                        pltpu.VMEM((2, SUBLANES + bm, bf), _f32),
                        pltpu.VMEM((bm, bf), _bf16),
                        pltpu.VMEM((bm, bf), _bf16),
                        pltpu.VMEM((nf, 2, SUBLANES, bf), _f32),
                        pltpu.VMEM((2, bm, 1), _f32)],
        compiler_params=_params(("arbitrary",)),
    )(hb, hf, w_up, w_up, conv_w, conv_w, conv_b2, conv_b2, w_down, g.reshape(1, d), b.reshape(1, d))


def _gelu(z):
    return 0.5 * z * (1.0 + lax.erf(z * (2.0 ** -0.5)))


def _identity(z):
    return z


def kernel(x, mix_in_a, norm_v_a_g, norm_v_a_b, sgu_w, sgu_b, mix_out_a, w_kv, mix_in_b, sinks,
           mix_out_b, ffn_up, ffn_conv_w, ffn_conv_b, ffn_down, ln_g, ln_b):
    batch, seq, d = x.shape
    depth = ffn_up.shape[0]
    n_a = mix_in_a.shape[0]
    alpha = (2.0 * depth) ** 0.25
    scale = HEAD_DIM ** -0.5
    m = batch * seq

    w_in_a = _cast_bf16(mix_in_a, block_rows=256)
    w_in_b = _cast_bf16(mix_in_b, block_rows=512)
    w_kv_b = _cast_bf16(w_kv[None], block_rows=1024)
    w_up = _cast_bf16(ffn_up, block_rows=64)
    w_down = _cast_bf16(ffn_down, block_rows=512)

    hf = x.reshape(m, d)
    hb = hf.astype(_bf16)
    kv = None
    for l in range(depth):
        if l < n_a:
            z = _mm_act(hb, w_in_a, l, _gelu, _f32, bm=1024, bn=512)
            mix_in = _sgu(z, norm_v_a_g[l], norm_v_a_b[l], sgu_w[l], sgu_b[l], bm=256)
            w_out = _cast_bf16_column_slabs(mix_out_a, l, bn=1024, block_rows=1024)
        else:
            j = l - n_a
            if kv is None:
                kv = _mm_act(hb, w_kv_b, 0, _identity, _bf16, bm=1024, bn=512)
            q = _mm_act(hb, w_in_b, j, lambda t: t * scale, _bf16, bm=1024, bn=512)
            mix_in = _attention(q, kv, sinks[j], batch, seq)
            w_out = _cast_bf16_column_slabs(mix_out_b, j, bn=1024, block_rows=1024)
        hf, hb = _mm_res_ln(mix_in, w_out, hf, ln_g[l, 0], ln_b[l, 0], alpha, bm=256)
        hf, hb = _ffn(hb, hf, w_up, w_down, l, ffn_conv_w[l], ffn_conv_b[l],
                      ln_g[l, 1], ln_b[l, 1], alpha, seq, bm=512, bf=256, bres=512)
    return hf.reshape(batch, seq, d)
```

```python
import functools

import jax
import jax.numpy as jnp
import numpy as np
from jax import lax
from jax.experimental import pallas as pl
from jax.experimental.pallas import tpu as pltpu

LN_EPS = 1e-5
CHUNK = 128
HEAD_DIM = 64
N_KV = 8
WINDOW = 128
CONV_W = 3

VMEM_LIMIT_BYTES = 56 * 1024 * 1024
SUBLANES = 8
LN_ROWS = 16
LN_UNROLL = 8
CONV_TILE_ROWS = 32

_f32 = jnp.float32
_bf16 = jnp.bfloat16


def _params(sem):
    return pltpu.CompilerParams(dimension_semantics=sem, vmem_limit_bytes=VMEM_LIMIT_BYTES)


def _layer_norm(y, g, b):
    mu = jnp.mean(y, axis=-1, keepdims=True)
    yc = y - mu
    var = jnp.mean(yc * yc, axis=-1, keepdims=True)
    return yc * lax.rsqrt(var + LN_EPS) * g + b


def _layer_norm_rows(o_ref, ob_ref, g_ref, b_ref, stat_ref):
    ngroups = o_ref.shape[0] // LN_ROWS

    def group(r):
        return pl.ds(pl.multiple_of(r * LN_ROWS, LN_ROWS), LN_ROWS)

    def mean_pass(r, carry):
        rows = group(r)
        stat_ref[0, rows, :] = jnp.mean(o_ref[rows, :], axis=-1, keepdims=True)
        return carry

    def var_pass(r, carry):
        rows = group(r)
        yc = o_ref[rows, :] - stat_ref[0, rows, :]
        stat_ref[1, rows, :] = lax.rsqrt(jnp.mean(yc * yc, axis=-1, keepdims=True) + LN_EPS)
        return carry

    def norm_pass(r, carry):
        rows = group(r)
        h = (o_ref[rows, :] - stat_ref[0, rows, :]) * stat_ref[1, rows, :] * g_ref[...] + b_ref[...]
        o_ref[rows, :] = h
        ob_ref[rows, :] = h.astype(_bf16)
        return carry

    for one_pass in (mean_pass, var_pass, norm_pass):
        lax.fori_loop(0, ngroups, one_pass, 0, unroll=LN_UNROLL)


def _mm_act_kernel(x_ref, w_ref, o_ref, *, act):
    z = jnp.dot(x_ref[...], w_ref[...], preferred_element_type=_f32)
    o_ref[...] = act(z).astype(o_ref.dtype)


def _mm_act(x, w, layer, act, out_dtype, bm, bn):
    m, k = x.shape
    n = w.shape[2]
    return pl.pallas_call(
        functools.partial(_mm_act_kernel, act=act),
        grid=(m // bm, n // bn),
        in_specs=[pl.BlockSpec((bm, k), lambda i, j: (i, 0)),
                  pl.BlockSpec((None, k, bn), lambda i, j: (layer, 0, j))],
        out_specs=pl.BlockSpec((bm, bn), lambda i, j: (i, j)),
        out_shape=jax.ShapeDtypeStruct((m, n), out_dtype),
        compiler_params=_params(("arbitrary", "arbitrary")),
    )(x, w)


def _cast_kernel(x_ref, o_ref):
    o_ref[...] = x_ref[...].astype(o_ref.dtype)


def _cast_bf16(w, block_rows):
    c = w.shape[-1]
    w2 = w.reshape(-1, c)
    r = w2.shape[0]
    assert r % block_rows == 0
    out = pl.pallas_call(
        _cast_kernel,
        grid=(r // block_rows,),
        in_specs=[pl.BlockSpec((block_rows, c), lambda i: (i, 0))],
        out_specs=pl.BlockSpec((block_rows, c), lambda i: (i, 0)),
        out_shape=jax.ShapeDtypeStruct((r, c), _bf16),
        compiler_params=_params(("arbitrary",)),
    )(w2)
    return out.reshape(w.shape)


def _cast_bf16_column_slabs(w, layer, bn, block_rows):
    _, k, n = w.shape
    return pl.pallas_call(
        _cast_kernel,
        grid=(k // block_rows, n // bn),
        in_specs=[pl.BlockSpec((None, block_rows, bn), lambda i, j: (layer, i, j))],
        out_specs=pl.BlockSpec((None, block_rows, bn), lambda i, j: (j, i, 0)),
        out_shape=jax.ShapeDtypeStruct((n // bn, k, bn), _bf16),
        compiler_params=_params(("arbitrary", "arbitrary")),
    )(w)


def _mm_res_ln_kernel(x_ref, w_ref, res_ref, g_ref, b_ref, o_ref, ob_ref, y0_ref, y1_ref,
                      *, alpha, nslab):
    i = pl.program_id(0)
    j = pl.program_id(1)
    bn = w_ref.shape[2]
    rows_out = o_ref.shape[0]

    @pl.when((i == 0) & (j == 0))
    def _():
        y1_ref[...] = jnp.zeros_like(y1_ref)

    def step(y_new, y_done):
        y_new[j] = alpha * res_ref[...] + jnp.dot(x_ref[...], w_ref[j], preferred_element_type=_f32)
        for r in range(0, rows_out, LN_ROWS):
            rows = pl.ds(pl.multiple_of(j * rows_out + r, LN_ROWS), LN_ROWS)
            parts = [y_done[s, rows, :] for s in range(nslab)]
            mu = sum(jnp.sum(p, axis=-1, keepdims=True) for p in parts) * (1.0 / (nslab * bn))
            parts = [p - mu for p in parts]
            var = sum(jnp.sum(p * p, axis=-1, keepdims=True) for p in parts) * (1.0 / (nslab * bn))
            inv = lax.rsqrt(var + LN_EPS)
            for s in range(nslab):
                cols = slice(s * bn, (s + 1) * bn)
                h = parts[s] * inv * g_ref[:, cols] + b_ref[:, cols]
                o_ref[r:r + LN_ROWS, cols] = h
                ob_ref[r:r + LN_ROWS, cols] = h.astype(_bf16)

    @pl.when(i % 2 == 0)
    def _():
        step(y0_ref, y1_ref)

    @pl.when(i % 2 == 1)
    def _():
        step(y1_ref, y0_ref)


def _mm_res_ln(x, w_slabs, res, g, b, alpha, bm):
    m, k = x.shape
    nslab, _, bn = w_slabs.shape
    n = nslab * bn
    nblk = m // bm
    rows_out = bm // nslab
    assert rows_out % LN_ROWS == 0

    def block_in(i):
        return jnp.minimum(i, nblk - 1)

    def out_map(i, j):
        return (jnp.maximum(i - 1, 0) * nslab + j, 0)

    return pl.pallas_call(
        functools.partial(_mm_res_ln_kernel, alpha=alpha, nslab=nslab),
        grid=(nblk + 1, nslab),
        in_specs=[pl.BlockSpec((bm, k), lambda i, j: (block_in(i), 0)),
                  pl.BlockSpec((nslab, k, bn), lambda i, j: (0, 0, 0), pipeline_mode=pl.Buffered(1)),
                  pl.BlockSpec((bm, bn), lambda i, j: (block_in(i), j)),
                  pl.BlockSpec((1, n), lambda i, j: (0, 0)),
                  pl.BlockSpec((1, n), lambda i, j: (0, 0))],
        out_specs=[pl.BlockSpec((rows_out, n), out_map),
                   pl.BlockSpec((rows_out, n), out_map)],
        out_shape=[jax.ShapeDtypeStruct((m, n), _f32),
                   jax.ShapeDtypeStruct((m, n), _bf16)],
        scratch_shapes=[pltpu.VMEM((nslab, bm, bn), _f32),
                        pltpu.VMEM((nslab, bm, bn), _f32)],
        compiler_params=_params(("arbitrary", "arbitrary")),
    )(x, w_slabs, res, g.reshape(1, n), b.reshape(1, n))


def _sgu_kernel(u_ref, v_ref, vg_ref, vb_ref, w_ref, bs_ref, o_ref, *, nchunk, ngroup):
    vn = _layer_norm(v_ref[...], vg_ref[...], vb_ref[...]).astype(_bf16)
    row = lax.broadcasted_iota(jnp.int32, (CHUNK, CHUNK), 0)
    col = lax.broadcasted_iota(jnp.int32, (CHUNK, CHUNK), 1)
    causal = col <= row
    for g in range(ngroup):
        lanes = slice(g * CHUNK, (g + 1) * CHUNK)
        wg = jnp.where(causal, w_ref[g], 0.0).astype(_bf16)
        rhs = jnp.concatenate(
            [vn[c * CHUNK:(c + 1) * CHUNK, lanes] for c in range(nchunk)], axis=1)
        vm = jnp.dot(wg, rhs, preferred_element_type=_f32) + bs_ref[g]
        for c in range(nchunk):
            rows = slice(c * CHUNK, (c + 1) * CHUNK)
            o_ref[rows, lanes] = (u_ref[rows, lanes] * vm[:, c * CHUNK:(c + 1) * CHUNK]).astype(_bf16)


def _sgu(z, vg, vb, w_s, b_s, bm):
    m, n2 = z.shape
    d = n2 // 2
    ngroup = d // CHUNK
    return pl.pallas_call(
        functools.partial(_sgu_kernel, nchunk=bm // CHUNK, ngroup=ngroup),
        grid=(m // bm,),
        in_specs=[pl.BlockSpec((bm, d), lambda i: (i, 0)),
                  pl.BlockSpec((bm, d), lambda i: (i, 1)),
                  pl.BlockSpec((1, d), lambda i: (0, 0)),
                  pl.BlockSpec((1, d), lambda i: (0, 0)),
                  pl.BlockSpec((ngroup, CHUNK, CHUNK), lambda i: (0, 0, 0)),
                  pl.BlockSpec((ngroup, CHUNK, 1), lambda i: (0, 0, 0))],
        out_specs=pl.BlockSpec((bm, d), lambda i: (i, 0)),
        out_shape=jax.ShapeDtypeStruct((m, d), _bf16),
        compiler_params=_params(("arbitrary",)),
    )(z, z, vg.reshape(1, d), vb.reshape(1, d), w_s, b_s.reshape(ngroup, CHUNK, 1))


def _attn_kernel(sink_ref, q_ref, kp_ref, kc_ref, vp_ref, vc_ref, o_ref, *, gqa):
    n = pl.program_id(1)
    blk = q_ref.shape[0]
    npair = gqa // 2
    pair_w = 2 * HEAD_DIM
    lower_kv = lax.broadcasted_iota(jnp.int32, (2 * blk, pair_w), 1) < HEAD_DIM
    lower_q = lax.broadcasted_iota(jnp.int32, (blk, pair_w), 1) < HEAD_DIM
    qi = lax.broadcasted_iota(jnp.int32, (blk, 2 * blk), 0)
    kj = lax.broadcasted_iota(jnp.int32, (blk, 2 * blk), 1)
    valid = (kj > qi) & (kj <= qi + WINDOW) & ((n > 0) | (kj >= blk))

    def halves(prev_ref, cur_ref, hp, fill):
        lanes = slice(hp * pair_w, (hp + 1) * pair_w)
        both = jnp.concatenate([prev_ref[:, lanes], cur_ref[:, lanes]], axis=0).astype(_f32)
        swapped = pltpu.roll(both, HEAD_DIM, 1)
        pad = jnp.full_like(both, fill)
        head_a = (jnp.where(lower_kv, both, pad), jnp.where(lower_kv, pad, swapped))
        head_b = (jnp.where(lower_kv, swapped, pad), jnp.where(lower_kv, pad, both))
        return [tuple(t.astype(_bf16) for t in head_a), tuple(t.astype(_bf16) for t in head_b)]

    for hp in range(N_KV // 2):
        k_heads = halves(kp_ref, kc_ref, hp, 0.0)
        v_heads = halves(vp_ref, vc_ref, hp, 1.0)
        for sub in range(2):
            h = 2 * hp + sub
            base = h * gqa * HEAD_DIM
            qs = jnp.concatenate(
                [q_ref[:, base + p * pair_w: base + (p + 1) * pair_w] for p in range(npair)], axis=0)
            pv, sink_terms = [], []
            for slot in range(2):
                s_all = lax.dot_general(qs, k_heads[sub][slot], (((1,), (1,)), ((), ())),
                                        preferred_element_type=_f32)
                e_slabs, terms = [], []
                for p in range(npair):
                    s = jnp.where(valid, s_all[p * blk:(p + 1) * blk], -jnp.inf)
                    sink = sink_ref[h * gqa + 2 * p + slot]
                    mx = jnp.maximum(jnp.max(s, axis=-1, keepdims=True), sink)
                    e_slabs.append(jnp.exp(s - mx).astype(_bf16))
                    terms.append(jnp.exp(sink - mx))
                pv.append(jnp.dot(jnp.concatenate(e_slabs, axis=0), v_heads[sub][slot],
                                  preferred_element_type=_f32))
                sink_terms.append(terms)
            for p in range(npair):
                a = pv[0][p * blk:(p + 1) * blk]
                b = pv[1][p * blk:(p + 1) * blk]
                num = jnp.where(lower_q, a, b)
                den = pltpu.roll(jnp.where(lower_q, b, a), HEAD_DIM, 1)
                den = den + jnp.where(lower_q, sink_terms[0][p], sink_terms[1][p])
                o_ref[:, base + p * pair_w: base + (p + 1) * pair_w] = (num / den).astype(o_ref.dtype)


def _attention(q, kv, sinks, batch, seq):
    m, d = q.shape
    blk = WINDOW
    nb = seq // blk
    kvw = N_KV * HEAD_DIM
    gqa = d // (N_KV * HEAD_DIM)

    def cur(col):
        return lambda b, n: (b * nb + n, col)

    def prev(col):
        return lambda b, n: (b * nb + jnp.maximum(n - 1, 0), col)

    return pl.pallas_call(
        functools.partial(_attn_kernel, gqa=gqa),
        grid=(batch, nb),
        in_specs=[pl.BlockSpec(memory_space=pltpu.SMEM),
                  pl.BlockSpec((blk, d), cur(0)),
                  pl.BlockSpec((blk, kvw), prev(0)),
                  pl.BlockSpec((blk, kvw), cur(0)),
                  pl.BlockSpec((blk, kvw), prev(1)),
                  pl.BlockSpec((blk, kvw), cur(1))],
        out_specs=pl.BlockSpec((blk, d), cur(0)),
        out_shape=jax.ShapeDtypeStruct((m, d), _bf16),
        compiler_params=_params(("arbitrary", "arbitrary")),
    )(sinks, q, kv, kv, kv, kv)


def _ffn_kernel(ia_ref, fa_ref, fb_ref, ic_ref, fc_ref, start_ref,
                hb_ref, hf_ref, wg_ref, wu_ref, cwg_ref, cwu_ref, cbg_ref, cbu_ref, wd_ref,
                g_ref, b_ref, o_ref, ob_ref, zs0_ref, zs1_ref, act0_ref, act1_ref, tail_ref, stat_ref,
                *, alpha, nf, nres):
    t = pl.program_id(0)
    bm = hb_ref.shape[0]
    res_rows = hf_ref.shape[0]
    fa = fa_ref[t]
    seq_start = start_ref[t] == 1
    fc = fc_ref[t]

    @pl.when(t == 0)
    def _():
        zs0_ref[...] = jnp.zeros_like(zs0_ref)
        zs1_ref[...] = jnp.zeros_like(zs1_ref)
        act0_ref[...] = jnp.zeros_like(act0_ref)
        act1_ref[...] = jnp.zeros_like(act1_ref)

    @pl.when(fc == 0)
    def _():
        o_ref[...] = jnp.zeros_like(o_ref)

    def conv_tile(zs_ref, half, cw_ref, cb_ref, r):
        rows = CONV_TILE_ROWS
        z0 = zs_ref[half, SUBLANES + r:SUBLANES + r + rows, :]
        z1 = zs_ref[half, SUBLANES - 1 + r:SUBLANES - 1 + r + rows, :]
        z2 = zs_ref[half, SUBLANES - 2 + r:SUBLANES - 2 + r + rows, :]
        return cw_ref[0:1, :] * z2 + cw_ref[1:2, :] * z1 + cw_ref[2:3, :] * z0 + cb_ref[...]

    def stages(zs_a, zs_b, act_b, act_c):
        hb = hb_ref[...]
        for half, w_ref in ((0, wg_ref), (1, wu_ref)):
            zs_a[half, :SUBLANES, :] = jnp.where(seq_start, 0.0, tail_ref[fa, half])
            zs_a[half, SUBLANES:, :] = jnp.dot(hb, w_ref[...], preferred_element_type=_f32)
            tail_ref[fa, half] = zs_a[half, bm:, :]
        ntile = bm // CONV_TILE_ROWS
        bn = o_ref.shape[1] // ntile
        for k in range(ntile):
            cols = slice(k * bn, (k + 1) * bn)
            o_ref[:, cols] += jnp.dot(act_c[...], wd_ref[:, cols], preferred_element_type=_f32)
            r = k * CONV_TILE_ROWS
            gate = conv_tile(zs_b, 0, cwg_ref, cbg_ref, r)
            up = conv_tile(zs_b, 1, cwu_ref, cbu_ref, r)
            act_b[r:r + CONV_TILE_ROWS, :] = (gate * jax.nn.sigmoid(gate) * up).astype(_bf16)

    @pl.when(t % 2 == 0)
    def _():
        stages(zs0_ref, zs1_ref, act1_ref, act0_ref)

    @pl.when(t % 2 == 1)
    def _():
        stages(zs1_ref, zs0_ref, act0_ref, act1_ref)

    @pl.when((fc < nres) & (t >= 2))
    def _():
        rows = pl.ds(pl.multiple_of(fc * res_rows, res_rows), res_rows)
        o_ref[rows, :] += alpha * hf_ref[...]

    @pl.when((fc == nf - 1) & (t >= 2))
    def _():
        _layer_norm_rows(o_ref, ob_ref, g_ref, b_ref, stat_ref)


def _ffn(hb, hf, w_up, w_down, layer, conv_w, conv_b, g, b, alpha, seq, bm, bf, nres):
    m, d = hb.shape
    d_ff = w_down.shape[1]
    nf = d_ff // bf
    nchunks = (m // bm) * nf
    res_rows = bm // nres
    assert nres <= nf and seq % bm == 0 and bm % CONV_TILE_ROWS == 0 and res_rows % SUBLANES == 0
    conv_b2 = conv_b.reshape(1, 2 * d_ff)

    steps = np.arange(nchunks + 2)
    chunk_a = np.minimum(steps, nchunks - 1)
    chunk_b = np.clip(steps - 1, 0, nchunks - 1)
    chunk_c = np.maximum(steps - 2, 0)
    tables = [chunk_a // nf, chunk_a % nf, chunk_b % nf, chunk_c // nf, chunk_c % nf,
              ((chunk_a // nf) % (seq // bm) == 0).astype(np.int32)]
    tables = [jnp.asarray(tbl, jnp.int32) for tbl in tables]

    def res_map(t, ia, fa, fb, ic, fc, start):
        return (ic[t] * nres + jnp.minimum(fc[t], nres - 1), 0)

    grid_spec = pltpu.PrefetchScalarGridSpec(
        num_scalar_prefetch=len(tables),
        grid=(nchunks + 2,),
        in_specs=[pl.BlockSpec((bm, d), lambda t, ia, fa, fb, ic, fc, start: (ia[t], 0)),
                  pl.BlockSpec((res_rows, d), res_map),
                  pl.BlockSpec((None, d, bf), lambda t, ia, fa, fb, ic, fc, start: (layer, 0, fa[t])),
                  pl.BlockSpec((None, d, bf), lambda t, ia, fa, fb, ic, fc, start: (layer, 0, nf + fa[t])),
                  pl.BlockSpec((CONV_W, bf), lambda t, ia, fa, fb, ic, fc, start: (0, fb[t])),
                  pl.BlockSpec((CONV_W, bf), lambda t, ia, fa, fb, ic, fc, start: (0, nf + fb[t])),
                  pl.BlockSpec((1, bf), lambda t, ia, fa, fb, ic, fc, start: (0, fb[t])),
                  pl.BlockSpec((1, bf), lambda t, ia, fa, fb, ic, fc, start: (0, nf + fb[t])),
                  pl.BlockSpec((None, bf, d), lambda t, ia, fa, fb, ic, fc, start: (layer, fc[t], 0)),
                  pl.BlockSpec((1, d), lambda t, ia, fa, fb, ic, fc, start: (0, 0)),
                  pl.BlockSpec((1, d), lambda t, ia, fa, fb, ic, fc, start: (0, 0))],
        out_specs=[pl.BlockSpec((bm, d), lambda t, ia, fa, fb, ic, fc, start: (ic[t], 0)),
                   pl.BlockSpec((bm, d), lambda t, ia, fa, fb, ic, fc, start: (ic[t], 0))],
        scratch_shapes=[pltpu.VMEM((2, SUBLANES + bm, bf), _f32),
                        pltpu.VMEM((2, SUBLANES + bm, bf), _f32),
                        pltpu.VMEM((bm, bf), _bf16),
                        pltpu.VMEM((bm, bf), _bf16),
                        pltpu.VMEM((nf, 2, SUBLANES, bf), _f32),
                        pltpu.VMEM((2, bm, 1), _f32)])
    return pl.pallas_call(
        functools.partial(_ffn_kernel, alpha=alpha, nf=nf, nres=nres),
        grid_spec=grid_spec,
        out_shape=[jax.ShapeDtypeStruct((m, d), _f32),
                   jax.ShapeDtypeStruct((m, d), _bf16)],
        compiler_params=_params(("arbitrary",)),
    )(*tables, hb, hf, w_up, w_up, conv_w, conv_w, conv_b2, conv_b2, w_down,
      g.reshape(1, d), b.reshape(1, d))


def _gelu(z):
    return 0.5 * z * (1.0 + lax.erf(z * (2.0 ** -0.5)))


def _identity(z):
    return z


def kernel(x, mix_in_a, norm_v_a_g, norm_v_a_b, sgu_w, sgu_b, mix_out_a, w_kv, mix_in_b, sinks,
           mix_out_b, ffn_up, ffn_conv_w, ffn_conv_b, ffn_down, ln_g, ln_b):
    batch, seq, d = x.shape
    depth = ffn_up.shape[0]
    n_a = mix_in_a.shape[0]
    alpha = (2.0 * depth) ** 0.25
    scale = HEAD_DIM ** -0.5
    m = batch * seq

    w_in_a = _cast_bf16(mix_in_a, block_rows=256)
    w_in_b = _cast_bf16(mix_in_b, block_rows=512)
    w_kv_b = _cast_bf16(w_kv[None], block_rows=1024)
    w_up = _cast_bf16(ffn_up, block_rows=64)
    w_down = _cast_bf16(ffn_down, block_rows=512)

    hf = x.reshape(m, d)
    hb = hf.astype(_bf16)
    kv = None
    for l in range(depth):
        if l < n_a:
            z = _mm_act(hb, w_in_a, l, _gelu, _f32, bm=1024, bn=512)
            mix_in = _sgu(z, norm_v_a_g[l], norm_v_a_b[l], sgu_w[l], sgu_b[l], bm=256)
            w_out = _cast_bf16_column_slabs(mix_out_a, l, bn=1024, block_rows=1024)
        else:
            j = l - n_a
            if kv is None:
                kv = _mm_act(hb, w_kv_b, 0, _identity, _bf16, bm=1024, bn=512)
            q = _mm_act(hb, w_in_b, j, lambda t: t * scale, _bf16, bm=1024, bn=512)
            mix_in = _attention(q, kv, sinks[j], batch, seq)
            w_out = _cast_bf16_column_slabs(mix_out_b, j, bn=1024, block_rows=1024)
        hf, hb = _mm_res_ln(mix_in, w_out, hf, ln_g[l, 0], ln_b[l, 0], alpha, bm=256)
        hf, hb = _ffn(hb, hf, w_up, w_down, l, ffn_conv_w[l], ffn_conv_b[l],
                      ln_g[l, 1], ln_b[l, 1], alpha, seq, bm=512, bf=256,
                      nres=min(8, ffn_down.shape[1] // 256))
    return hf.reshape(batch, seq, d)
```

```python
import functools

import jax
import jax.numpy as jnp
import numpy as np
from jax import lax
from jax.experimental import pallas as pl
from jax.experimental.pallas import tpu as pltpu

LN_EPS = 1e-5
CHUNK = 128
HEAD_DIM = 64
N_KV = 8
WINDOW = 128
CONV_W = 3

VMEM_LIMIT_BYTES = 56 * 1024 * 1024
SUBLANES = 8
LN_ROWS = 16
LN_UNROLL = 8
LN_COLS = 1024
CONV_TILE_ROWS = 32

_f32 = jnp.float32
_bf16 = jnp.bfloat16


def _params(sem):
    return pltpu.CompilerParams(dimension_semantics=sem, vmem_limit_bytes=VMEM_LIMIT_BYTES)


def _layer_norm(y, g, b):
    mu = jnp.mean(y, axis=-1, keepdims=True)
    yc = y - mu
    var = jnp.mean(yc * yc, axis=-1, keepdims=True)
    return yc * lax.rsqrt(var + LN_EPS) * g + b


def _layer_norm_rows(o_ref, ob_ref, gb_ref, stat_ref):
    ngroups = o_ref.shape[0] // LN_ROWS

    def group(r):
        return pl.ds(pl.multiple_of(r * LN_ROWS, LN_ROWS), LN_ROWS)

    def mean_pass(r, carry):
        rows = group(r)
        stat_ref[0, rows, :] = jnp.mean(o_ref[rows, :], axis=-1, keepdims=True)
        return carry

    n = o_ref.shape[1]
    slabs = [slice(c, c + LN_COLS) for c in range(0, n, LN_COLS)]

    def var_pass(r, carry):
        rows = group(r)
        mu = stat_ref[0, rows, :]
        sq = None
        for cols in slabs:
            yc = o_ref[rows, cols] - mu
            sq = yc * yc if sq is None else sq + yc * yc
        ssq = jnp.sum(sq, axis=-1, keepdims=True)
        stat_ref[1, rows, :] = lax.rsqrt(ssq * (1.0 / n) + LN_EPS)
        return carry

    def norm_pass(r, carry):
        rows = group(r)
        mu = stat_ref[0, rows, :]
        inv = stat_ref[1, rows, :]
        for cols in slabs:
            h = (o_ref[rows, cols] - mu) * inv * gb_ref[0:1, cols] + gb_ref[1:2, cols]
            o_ref[rows, cols] = h
            ob_ref[rows, cols] = h.astype(_bf16)
        return carry

    for one_pass in (mean_pass, var_pass, norm_pass):
        lax.fori_loop(0, ngroups, one_pass, 0, unroll=LN_UNROLL)


def _mm_act_kernel(x_ref, w_ref, o_ref, *, act):
    z = jnp.dot(x_ref[...], w_ref[...], preferred_element_type=_f32)
    o_ref[...] = act(z).astype(o_ref.dtype)


def _mm_act(x, w, layer, act, out_dtype, bm, bn):
    m, k = x.shape
    n = w.shape[2]
    return pl.pallas_call(
        functools.partial(_mm_act_kernel, act=act),
        grid=(m // bm, n // bn),
        in_specs=[pl.BlockSpec((bm, k), lambda i, j: (i, 0)),
                  pl.BlockSpec((None, k, bn), lambda i, j: (layer, 0, j))],
        out_specs=pl.BlockSpec((bm, bn), lambda i, j: (i, j)),
        out_shape=jax.ShapeDtypeStruct((m, n), out_dtype),
        compiler_params=_params(("arbitrary", "arbitrary")),
    )(x, w)


def _cast_kernel(x_ref, o_ref):
    o_ref[...] = x_ref[...].astype(o_ref.dtype)


def _cast_bf16(w, block_rows):
    c = w.shape[-1]
    w2 = w.reshape(-1, c)
    r = w2.shape[0]
    assert r % block_rows == 0
    out = pl.pallas_call(
        _cast_kernel,
        grid=(r // block_rows,),
        in_specs=[pl.BlockSpec((block_rows, c), lambda i: (i, 0))],
        out_specs=pl.BlockSpec((block_rows, c), lambda i: (i, 0)),
        out_shape=jax.ShapeDtypeStruct((r, c), _bf16),
        compiler_params=_params(("arbitrary",)),
    )(w2)
    return out.reshape(w.shape)


def _cast_bf16_column_slabs(w, layer, bn, block_rows):
    _, k, n = w.shape
    return pl.pallas_call(
        _cast_kernel,
        grid=(k // block_rows, n // bn),
        in_specs=[pl.BlockSpec((None, block_rows, bn), lambda i, j: (layer, i, j))],
        out_specs=pl.BlockSpec((None, block_rows, bn), lambda i, j: (j, i, 0)),
        out_shape=jax.ShapeDtypeStruct((n // bn, k, bn), _bf16),
        compiler_params=_params(("arbitrary", "arbitrary")),
    )(w)


def _mm_res_ln_kernel(x_ref, w_ref, res_ref, g_ref, b_ref, o_ref, ob_ref, y0_ref, y1_ref,
                      *, alpha, nslab):
    i = pl.program_id(0)
    j = pl.program_id(1)
    bn = w_ref.shape[2]
    rows_out = o_ref.shape[0]

    @pl.when((i == 0) & (j == 0))
    def _():
        y1_ref[...] = jnp.zeros_like(y1_ref)

    def step(y_new, y_done):
        y_new[j] = alpha * res_ref[...] + jnp.dot(x_ref[...], w_ref[j], preferred_element_type=_f32)
        for r in range(0, rows_out, LN_ROWS):
            rows = pl.ds(pl.multiple_of(j * rows_out + r, LN_ROWS), LN_ROWS)
            parts = [y_done[s, rows, :] for s in range(nslab)]
            mu = sum(jnp.sum(p, axis=-1, keepdims=True) for p in parts) * (1.0 / (nslab * bn))
            parts = [p - mu for p in parts]
            var = sum(jnp.sum(p * p, axis=-1, keepdims=True) for p in parts) * (1.0 / (nslab * bn))
            inv = lax.rsqrt(var + LN_EPS)
            for s in range(nslab):
                cols = slice(s * bn, (s + 1) * bn)
                h = parts[s] * inv * g_ref[:, cols] + b_ref[:, cols]
                o_ref[r:r + LN_ROWS, cols] = h
                ob_ref[r:r + LN_ROWS, cols] = h.astype(_bf16)

    @pl.when(i % 2 == 0)
    def _():
        step(y0_ref, y1_ref)

    @pl.when(i % 2 == 1)
    def _():
        step(y1_ref, y0_ref)


def _mm_res_ln(x, w_slabs, res, g, b, alpha, bm):
    m, k = x.shape
    nslab, _, bn = w_slabs.shape
    n = nslab * bn
    nblk = m // bm
    rows_out = bm // nslab
    assert rows_out % LN_ROWS == 0

    def block_in(i):
        return jnp.minimum(i, nblk - 1)

    def out_map(i, j):
        return (jnp.maximum(i - 1, 0) * nslab + j, 0)

    return pl.pallas_call(
        functools.partial(_mm_res_ln_kernel, alpha=alpha, nslab=nslab),
        grid=(nblk + 1, nslab),
        in_specs=[pl.BlockSpec((bm, k), lambda i, j: (block_in(i), 0)),
                  pl.BlockSpec((nslab, k, bn), lambda i, j: (0, 0, 0), pipeline_mode=pl.Buffered(1)),
                  pl.BlockSpec((bm, bn), lambda i, j: (block_in(i), j)),
                  pl.BlockSpec((1, n), lambda i, j: (0, 0)),
                  pl.BlockSpec((1, n), lambda i, j: (0, 0))],
        out_specs=[pl.BlockSpec((rows_out, n), out_map),
                   pl.BlockSpec((rows_out, n), out_map)],
        out_shape=[jax.ShapeDtypeStruct((m, n), _f32),
                   jax.ShapeDtypeStruct((m, n), _bf16)],
        scratch_shapes=[pltpu.VMEM((nslab, bm, bn), _f32),
                        pltpu.VMEM((nslab, bm, bn), _f32)],
        compiler_params=_params(("arbitrary", "arbitrary")),
    )(x, w_slabs, res, g.reshape(1, n), b.reshape(1, n))


def _sgu_kernel(u_ref, v_ref, vg_ref, vb_ref, w_ref, bs_ref, o_ref, *, nchunk, ngroup):
    vn = _layer_norm(v_ref[...].astype(_f32), vg_ref[...], vb_ref[...]).astype(_bf16)
    row = lax.broadcasted_iota(jnp.int32, (CHUNK, CHUNK), 0)
    col = lax.broadcasted_iota(jnp.int32, (CHUNK, CHUNK), 1)
    causal = col <= row
    for g in range(ngroup):
        lanes = slice(g * CHUNK, (g + 1) * CHUNK)
        wg = jnp.where(causal, w_ref[g], 0.0).astype(_bf16)
        rhs = jnp.concatenate(
            [vn[c * CHUNK:(c + 1) * CHUNK, lanes] for c in range(nchunk)], axis=1)
        vm = jnp.dot(wg, rhs, preferred_element_type=_f32) + bs_ref[g]
        for c in range(nchunk):
            rows = slice(c * CHUNK, (c + 1) * CHUNK)
            o_ref[rows, lanes] = (u_ref[rows, lanes].astype(_f32)
                                  * vm[:, c * CHUNK:(c + 1) * CHUNK]).astype(_bf16)


def _sgu(z, vg, vb, w_s, b_s, bm):
    m, n2 = z.shape
    d = n2 // 2
    ngroup = d // CHUNK
    return pl.pallas_call(
        functools.partial(_sgu_kernel, nchunk=bm // CHUNK, ngroup=ngroup),
        grid=(m // bm,),
        in_specs=[pl.BlockSpec((bm, d), lambda i: (i, 0)),
                  pl.BlockSpec((bm, d), lambda i: (i, 1)),
                  pl.BlockSpec((1, d), lambda i: (0, 0)),
                  pl.BlockSpec((1, d), lambda i: (0, 0)),
                  pl.BlockSpec((ngroup, CHUNK, CHUNK), lambda i: (0, 0, 0)),
                  pl.BlockSpec((ngroup, CHUNK, 1), lambda i: (0, 0, 0))],
        out_specs=pl.BlockSpec((bm, d), lambda i: (i, 0)),
        out_shape=jax.ShapeDtypeStruct((m, d), _bf16),
        compiler_params=_params(("arbitrary",)),
    )(z, z, vg.reshape(1, d), vb.reshape(1, d), w_s, b_s.reshape(ngroup, CHUNK, 1))


def _attn_kernel(sink_ref, q_ref, kp_ref, kc_ref, vp_ref, vc_ref, o_ref, *, gqa):
    n = pl.program_id(1)
    blk = q_ref.shape[0]
    npair = gqa // 2
    pair_w = 2 * HEAD_DIM
    lower_kv = lax.broadcasted_iota(jnp.int32, (2 * blk, pair_w), 1) < HEAD_DIM
    lower_q = lax.broadcasted_iota(jnp.int32, (blk, pair_w), 1) < HEAD_DIM
    qi = lax.broadcasted_iota(jnp.int32, (blk, 2 * blk), 0)
    kj = lax.broadcasted_iota(jnp.int32, (blk, 2 * blk), 1)
    valid = (kj > qi) & (kj <= qi + WINDOW) & ((n > 0) | (kj >= blk))

    def halves(prev_ref, cur_ref, hp, fill):
        lanes = slice(hp * pair_w, (hp + 1) * pair_w)
        both = jnp.concatenate([prev_ref[:, lanes], cur_ref[:, lanes]], axis=0).astype(_f32)
        swapped = pltpu.roll(both, HEAD_DIM, 1)
        pad = jnp.full_like(both, fill)
        head_a = (jnp.where(lower_kv, both, pad), jnp.where(lower_kv, pad, swapped))
        head_b = (jnp.where(lower_kv, swapped, pad), jnp.where(lower_kv, pad, both))
        return [tuple(t.astype(_bf16) for t in head_a), tuple(t.astype(_bf16) for t in head_b)]

    for hp in range(N_KV // 2):
        k_heads = halves(kp_ref, kc_ref, hp, 0.0)
        v_heads = halves(vp_ref, vc_ref, hp, 1.0)
        for sub in range(2):
            h = 2 * hp + sub
            base = h * gqa * HEAD_DIM
            qs = jnp.concatenate(
                [q_ref[:, base + p * pair_w: base + (p + 1) * pair_w] for p in range(npair)], axis=0)
            pv, sink_terms = [], []
            for slot in range(2):
                s_all = lax.dot_general(qs, k_heads[sub][slot], (((1,), (1,)), ((), ())),
                                        preferred_element_type=_f32)
                e_slabs, terms = [], []
                for p in range(npair):
                    s = jnp.where(valid, s_all[p * blk:(p + 1) * blk], -jnp.inf)
                    sink = sink_ref[h * gqa + 2 * p + slot]
                    mx = jnp.maximum(jnp.max(s, axis=-1, keepdims=True), sink)
                    e_slabs.append(jnp.exp(s - mx).astype(_bf16))
                    terms.append(jnp.exp(sink - mx))
                pv.append(jnp.dot(jnp.concatenate(e_slabs, axis=0), v_heads[sub][slot],
                                  preferred_element_type=_f32))
                sink_terms.append(terms)
            for p in range(npair):
                a = pv[0][p * blk:(p + 1) * blk]
                b = pv[1][p * blk:(p + 1) * blk]
                num = jnp.where(lower_q, a, b)
                den = pltpu.roll(jnp.where(lower_q, b, a), HEAD_DIM, 1)
                den = den + jnp.where(lower_q, sink_terms[0][p], sink_terms[1][p])
                o_ref[:, base + p * pair_w: base + (p + 1) * pair_w] = (num / den).astype(o_ref.dtype)


def _attention(q, kv, sinks, batch, seq):
    m, d = q.shape
    blk = WINDOW
    nb = seq // blk
    kvw = N_KV * HEAD_DIM
    gqa = d // (N_KV * HEAD_DIM)

    def cur(col):
        return lambda b, n: (b * nb + n, col)

    def prev(col):
        return lambda b, n: (b * nb + jnp.maximum(n - 1, 0), col)

    return pl.pallas_call(
        functools.partial(_attn_kernel, gqa=gqa),
        grid=(batch, nb),
        in_specs=[pl.BlockSpec(memory_space=pltpu.SMEM),
                  pl.BlockSpec((blk, d), cur(0)),
                  pl.BlockSpec((blk, kvw), prev(0)),
                  pl.BlockSpec((blk, kvw), cur(0)),
                  pl.BlockSpec((blk, kvw), prev(1)),
                  pl.BlockSpec((blk, kvw), cur(1))],
        out_specs=pl.BlockSpec((blk, d), cur(0)),
        out_shape=jax.ShapeDtypeStruct((m, d), _bf16),
        compiler_params=_params(("arbitrary", "arbitrary")),
    )(sinks, q, kv, kv, kv, kv)


def _ffn_kernel(ia_ref, fa_ref, fb_ref, ic_ref, fc_ref, start_ref,
                hb_ref, hf_ref, wg_ref, wu_ref, cp_ref, wd_ref, gb_ref,
                o_ref, ob_ref, zs0_ref, zs1_ref, act0_ref, act1_ref, tail_ref, stat_ref,
                *, alpha, nf, nres):
    t = pl.program_id(0)
    bm = hb_ref.shape[0]
    res_rows = hf_ref.shape[0]
    fa = fa_ref[t]
    seq_start = start_ref[t] == 1
    fc = fc_ref[t]

    @pl.when(t == 0)
    def _():
        zs0_ref[...] = jnp.zeros_like(zs0_ref)
        zs1_ref[...] = jnp.zeros_like(zs1_ref)
        act0_ref[...] = jnp.zeros_like(act0_ref)
        act1_ref[...] = jnp.zeros_like(act1_ref)

    @pl.when(fc == 0)
    def _():
        o_ref[...] = jnp.zeros_like(o_ref)

    bf = wg_ref.shape[1]

    def conv_tile(zs_ref, half, r):
        rows = CONV_TILE_ROWS
        lanes = slice(half * bf, (half + 1) * bf)
        z0 = zs_ref[half, SUBLANES + r:SUBLANES + r + rows, :]
        z1 = zs_ref[half, SUBLANES - 1 + r:SUBLANES - 1 + r + rows, :]
        z2 = zs_ref[half, SUBLANES - 2 + r:SUBLANES - 2 + r + rows, :]
        return (cp_ref[0:1, lanes] * z2 + cp_ref[1:2, lanes] * z1 + cp_ref[2:3, lanes] * z0
                + cp_ref[3:4, lanes])

    def stages(zs_a, zs_b, act_b, act_c):
        hb = hb_ref[...]
        for half, w_ref in ((0, wg_ref), (1, wu_ref)):
            zs_a[half, :SUBLANES, :] = jnp.where(seq_start, 0.0, tail_ref[fa, half])
            zs_a[half, SUBLANES:, :] = jnp.dot(hb, w_ref[...], preferred_element_type=_f32)
            tail_ref[fa, half] = zs_a[half, bm:, :]
        ntile = bm // CONV_TILE_ROWS
        bn = o_ref.shape[1] // ntile
        for k in range(ntile):
            cols = slice(k * bn, (k + 1) * bn)
            o_ref[:, cols] += jnp.dot(act_c[...], wd_ref[:, cols], preferred_element_type=_f32)
            r = k * CONV_TILE_ROWS
            gate = conv_tile(zs_b, 0, r)
            up = conv_tile(zs_b, 1, r)
            act_b[r:r + CONV_TILE_ROWS, :] = (gate * jax.nn.sigmoid(gate) * up).astype(_bf16)

    @pl.when(t % 2 == 0)
    def _():
        stages(zs0_ref, zs1_ref, act1_ref, act0_ref)

    @pl.when(t % 2 == 1)
    def _():
        stages(zs1_ref, zs0_ref, act0_ref, act1_ref)

    @pl.when((fc < nres) & (t >= 2))
    def _():
        rows = pl.ds(pl.multiple_of(fc * res_rows, res_rows), res_rows)
        o_ref[rows, :] += alpha * hf_ref[...]

    @pl.when((fc == nf - 1) & (t >= 2))
    def _():
        _layer_norm_rows(o_ref, ob_ref, gb_ref, stat_ref)


def _ffn(hb, hf, w_up, w_down, layer, conv_w, conv_b, g, b, alpha, seq, bm, bf, nres):
    m, d = hb.shape
    d_ff = w_down.shape[1]
    nf = d_ff // bf
    nchunks = (m // bm) * nf
    res_rows = bm // nres
    assert nres <= nf and seq % bm == 0 and bm % CONV_TILE_ROWS == 0 and res_rows % SUBLANES == 0
    taps = jnp.concatenate([conv_w, conv_b[None, :]], axis=0).reshape(CONV_W + 1, 2, nf, bf)
    conv_params = jnp.pad(taps.transpose(2, 0, 1, 3).reshape(nf, CONV_W + 1, 2 * bf),
                          ((0, 0), (0, SUBLANES - CONV_W - 1), (0, 0)))
    gain_bias = jnp.stack([g, b])

    steps = np.arange(nchunks + 2)
    chunk_a = np.minimum(steps, nchunks - 1)
    chunk_b = np.clip(steps - 1, 0, nchunks - 1)
    chunk_c = np.maximum(steps - 2, 0)
    tables = [chunk_a // nf, chunk_a % nf, chunk_b % nf, chunk_c // nf, chunk_c % nf,
              ((chunk_a // nf) % (seq // bm) == 0).astype(np.int32)]
    tables = [jnp.asarray(tbl, jnp.int32) for tbl in tables]

    def res_map(t, ia, fa, fb, ic, fc, start):
        return (ic[t] * nres + jnp.minimum(fc[t], nres - 1), 0)

    grid_spec = pltpu.PrefetchScalarGridSpec(
        num_scalar_prefetch=len(tables),
        grid=(nchunks + 2,),
        in_specs=[pl.BlockSpec((bm, d), lambda t, ia, fa, fb, ic, fc, start: (ia[t], 0)),
                  pl.BlockSpec((res_rows, d), res_map),
                  pl.BlockSpec((None, d, bf), lambda t, ia, fa, fb, ic, fc, start: (layer, 0, fa[t])),
                  pl.BlockSpec((None, d, bf), lambda t, ia, fa, fb, ic, fc, start: (layer, 0, nf + fa[t])),
                  pl.BlockSpec((None, SUBLANES, 2 * bf),
                               lambda t, ia, fa, fb, ic, fc, start: (fb[t], 0, 0)),
                  pl.BlockSpec((None, bf, d), lambda t, ia, fa, fb, ic, fc, start: (layer, fc[t], 0)),
                  pl.BlockSpec((2, d), lambda t, ia, fa, fb, ic, fc, start: (0, 0))],
        out_specs=[pl.BlockSpec((bm, d), lambda t, ia, fa, fb, ic, fc, start: (ic[t], 0)),
                   pl.BlockSpec((bm, d), lambda t, ia, fa, fb, ic, fc, start: (ic[t], 0))],
        scratch_shapes=[pltpu.VMEM((2, SUBLANES + bm, bf), _f32),
                        pltpu.VMEM((2, SUBLANES + bm, bf), _f32),
                        pltpu.VMEM((bm, bf), _bf16),
                        pltpu.VMEM((bm, bf), _bf16),
                        pltpu.VMEM((nf, 2, SUBLANES, bf), _f32),
                        pltpu.VMEM((2, bm, 1), _f32)])
    return pl.pallas_call(
        functools.partial(_ffn_kernel, alpha=alpha, nf=nf, nres=nres),
        grid_spec=grid_spec,
        out_shape=[jax.ShapeDtypeStruct((m, d), _f32),
                   jax.ShapeDtypeStruct((m, d), _bf16)],
        compiler_params=_params(("arbitrary",)),
    )(*tables, hb, hf, w_up, w_up, conv_params, w_down, gain_bias)


def _gelu(z):
    return 0.5 * z * (1.0 + lax.erf(z * (2.0 ** -0.5)))


def _identity(z):
    return z


def kernel(x, mix_in_a, norm_v_a_g, norm_v_a_b, sgu_w, sgu_b, mix_out_a, w_kv, mix_in_b, sinks,
           mix_out_b, ffn_up, ffn_conv_w, ffn_conv_b, ffn_down, ln_g, ln_b):
    batch, seq, d = x.shape
    depth = ffn_up.shape[0]
    n_a = mix_in_a.shape[0]
    alpha = (2.0 * depth) ** 0.25
    scale = HEAD_DIM ** -0.5
    m = batch * seq

    w_in_a = _cast_bf16(mix_in_a, block_rows=256)
    w_in_b = _cast_bf16(mix_in_b, block_rows=512)
    w_kv_b = _cast_bf16(w_kv[None], block_rows=1024)
    w_up = _cast_bf16(ffn_up, block_rows=64)
    w_down = _cast_bf16(ffn_down, block_rows=512)

    out_slab = min(1024, d)

    hf = x.reshape(m, d)
    hb = hf.astype(_bf16)
    kv = None
    for l in range(depth):
        if l < n_a:
            z = _mm_act(hb, w_in_a, l, _gelu, _bf16, bm=1024, bn=1024)
            mix_in = _sgu(z, norm_v_a_g[l], norm_v_a_b[l], sgu_w[l], sgu_b[l], bm=256)
            w_out = _cast_bf16_column_slabs(mix_out_a, l, bn=out_slab, block_rows=512)
        else:
            j = l - n_a
            if kv is None:
                kv = _mm_act(hb, w_kv_b, 0, _identity, _bf16, bm=1024, bn=1024)
            q = _mm_act(hb, w_in_b, j, lambda t: t * scale, _bf16, bm=1024, bn=1024)
            mix_in = _attention(q, kv, sinks[j], batch, seq)
            w_out = _cast_bf16_column_slabs(mix_out_b, j, bn=out_slab, block_rows=512)
        hf, hb = _mm_res_ln(mix_in, w_out, hf, ln_g[l, 0], ln_b[l, 0], alpha, bm=256)
        hf, hb = _ffn(hb, hf, w_up, w_down, l, ffn_conv_w[l], ffn_conv_b[l],
                      ln_g[l, 1], ln_b[l, 1], alpha, seq, bm=512, bf=256,
                      nres=min(8, ffn_down.shape[1] // 256))
    return hf.reshape(batch, seq, d)
```

```python
import functools

import jax
import jax.numpy as jnp
import numpy as np
from jax import lax
from jax.experimental import pallas as pl
from jax.experimental.pallas import tpu as pltpu

LN_EPS = 1e-5
CHUNK = 128
HEAD_DIM = 64
N_KV = 8
WINDOW = 128
CONV_W = 3

VMEM_LIMIT_BYTES = 56 * 1024 * 1024
SUBLANES = 8
LN_ROWS = 16
LN_UNROLL = 8
LN_COLS = 1024
CONV_TILE_ROWS = 32
DOWN_SLAB_COLS = 256
MM_ROWS = 512

_f32 = jnp.float32
_bf16 = jnp.bfloat16


def _params(sem):
    return pltpu.CompilerParams(dimension_semantics=sem, vmem_limit_bytes=VMEM_LIMIT_BYTES)


def _layer_norm(y, g, b):
    mu = jnp.mean(y, axis=-1, keepdims=True)
    yc = y - mu
    var = jnp.mean(yc * yc, axis=-1, keepdims=True)
    return yc * lax.rsqrt(var + LN_EPS) * g + b


def _layer_norm_rows(o_ref, ob_ref, gb_ref, stat_ref):
    ngroups = o_ref.shape[0] // LN_ROWS

    def group(r):
        return pl.ds(pl.multiple_of(r * LN_ROWS, LN_ROWS), LN_ROWS)

    def mean_pass(r, carry):
        rows = group(r)
        stat_ref[0, rows, :] = jnp.mean(o_ref[rows, :], axis=-1, keepdims=True)
        return carry

    n = o_ref.shape[1]
    slabs = [slice(c, c + LN_COLS) for c in range(0, n, LN_COLS)]

    def var_pass(r, carry):
        rows = group(r)
        mu = stat_ref[0, rows, :]
        sq = None
        for cols in slabs:
            yc = o_ref[rows, cols] - mu
            sq = yc * yc if sq is None else sq + yc * yc
        ssq = jnp.sum(sq, axis=-1, keepdims=True)
        stat_ref[1, rows, :] = lax.rsqrt(ssq * (1.0 / n) + LN_EPS)
        return carry

    def norm_pass(r, carry):
        rows = group(r)
        mu = stat_ref[0, rows, :]
        inv = stat_ref[1, rows, :]
        for cols in slabs:
            h = (o_ref[rows, cols] - mu) * inv * gb_ref[0:1, cols] + gb_ref[1:2, cols]
            o_ref[rows, cols] = h
            ob_ref[rows, cols] = h.astype(_bf16)
        return carry

    for one_pass in (mean_pass, var_pass, norm_pass):
        lax.fori_loop(0, ngroups, one_pass, 0, unroll=LN_UNROLL)


def _mm_act_kernel(x_ref, w_ref, o_ref, *, act):
    z = jnp.dot(x_ref[...], w_ref[...], preferred_element_type=_f32)
    o_ref[...] = act(z).astype(o_ref.dtype)


def _mm_act(x, w, layer, act, out_dtype, bm, bn):
    m, k = x.shape
    n = w.shape[2]
    return pl.pallas_call(
        functools.partial(_mm_act_kernel, act=act),
        grid=(m // bm, n // bn),
        in_specs=[pl.BlockSpec((bm, k), lambda i, j: (i, 0)),
                  pl.BlockSpec((None, k, bn), lambda i, j: (layer, 0, j))],
        out_specs=pl.BlockSpec((bm, bn), lambda i, j: (i, j)),
        out_shape=jax.ShapeDtypeStruct((m, n), out_dtype),
        compiler_params=_params(("arbitrary", "arbitrary")),
    )(x, w)


def _cast_kernel(x_ref, o_ref):
    o_ref[...] = x_ref[...].astype(o_ref.dtype)


def _cast_bf16(w, block_rows):
    c = w.shape[-1]
    w2 = w.reshape(-1, c)
    r = w2.shape[0]
    assert r % block_rows == 0
    out = pl.pallas_call(
        _cast_kernel,
        grid=(r // block_rows,),
        in_specs=[pl.BlockSpec((block_rows, c), lambda i: (i, 0))],
        out_specs=pl.BlockSpec((block_rows, c), lambda i: (i, 0)),
        out_shape=jax.ShapeDtypeStruct((r, c), _bf16),
        compiler_params=_params(("arbitrary",)),
    )(w2)
    return out.reshape(w.shape)


def _cast_bf16_column_slabs(w, layer, bn, block_rows):
    _, k, n = w.shape
    return pl.pallas_call(
        _cast_kernel,
        grid=(k // block_rows, n // bn),
        in_specs=[pl.BlockSpec((None, block_rows, bn), lambda i, j: (layer, i, j))],
        out_specs=pl.BlockSpec((None, block_rows, bn), lambda i, j: (j, i, 0)),
        out_shape=jax.ShapeDtypeStruct((n // bn, k, bn), _bf16),
        compiler_params=_params(("arbitrary", "arbitrary")),
    )(w)


def _mm_res_ln_kernel(x_ref, w_ref, res_ref, g_ref, b_ref, o_ref, ob_ref, y0_ref, y1_ref,
                      *, alpha, nslab):
    i = pl.program_id(0)
    j = pl.program_id(1)
    bn = w_ref.shape[2]
    rows_out = o_ref.shape[0]

    @pl.when((i == 0) & (j == 0))
    def _():
        y1_ref[...] = jnp.zeros_like(y1_ref)

    def step(y_new, y_done):
        y_new[j] = alpha * res_ref[...] + jnp.dot(x_ref[...], w_ref[j], preferred_element_type=_f32)
        for r in range(0, rows_out, LN_ROWS):
            rows = pl.ds(pl.multiple_of(j * rows_out + r, LN_ROWS), LN_ROWS)
            parts = [y_done[s, rows, :] for s in range(nslab)]
            mu = sum(jnp.sum(p, axis=-1, keepdims=True) for p in parts) * (1.0 / (nslab * bn))
            parts = [p - mu for p in parts]
            var = sum(jnp.sum(p * p, axis=-1, keepdims=True) for p in parts) * (1.0 / (nslab * bn))
            inv = lax.rsqrt(var + LN_EPS)
            for s in range(nslab):
                cols = slice(s * bn, (s + 1) * bn)
                h = parts[s] * inv * g_ref[:, cols] + b_ref[:, cols]
                o_ref[r:r + LN_ROWS, cols] = h
                ob_ref[r:r + LN_ROWS, cols] = h.astype(_bf16)

    @pl.when(i % 2 == 0)
    def _():
        step(y0_ref, y1_ref)

    @pl.when(i % 2 == 1)
    def _():
        step(y1_ref, y0_ref)


def _mm_res_ln(x, w_slabs, res, g, b, alpha, bm):
    m, k = x.shape
    nslab, _, bn = w_slabs.shape
    n = nslab * bn
    nblk = m // bm
    rows_out = bm // nslab
    assert rows_out % LN_ROWS == 0

    def block_in(i):
        return jnp.minimum(i, nblk - 1)

    def out_map(i, j):
        return (jnp.maximum(i - 1, 0) * nslab + j, 0)

    return pl.pallas_call(
        functools.partial(_mm_res_ln_kernel, alpha=alpha, nslab=nslab),
        grid=(nblk + 1, nslab),
        in_specs=[pl.BlockSpec((bm, k), lambda i, j: (block_in(i), 0)),
                  pl.BlockSpec((nslab, k, bn), lambda i, j: (0, 0, 0), pipeline_mode=pl.Buffered(1)),
                  pl.BlockSpec((bm, bn), lambda i, j: (block_in(i), j)),
                  pl.BlockSpec((1, n), lambda i, j: (0, 0)),
                  pl.BlockSpec((1, n), lambda i, j: (0, 0))],
        out_specs=[pl.BlockSpec((rows_out, n), out_map),
                   pl.BlockSpec((rows_out, n), out_map)],
        out_shape=[jax.ShapeDtypeStruct((m, n), _f32),
                   jax.ShapeDtypeStruct((m, n), _bf16)],
        scratch_shapes=[pltpu.VMEM((nslab, bm, bn), _f32),
                        pltpu.VMEM((nslab, bm, bn), _f32)],
        compiler_params=_params(("arbitrary", "arbitrary")),
    )(x, w_slabs, res, g.reshape(1, n), b.reshape(1, n))


def _sgu_kernel(u_ref, v_ref, vg_ref, vb_ref, w_ref, bs_ref, o_ref, *, nchunk, ngroup):
    vn = _layer_norm(v_ref[...].astype(_f32), vg_ref[...], vb_ref[...]).astype(_bf16)
    row = lax.broadcasted_iota(jnp.int32, (CHUNK, CHUNK), 0)
    col = lax.broadcasted_iota(jnp.int32, (CHUNK, CHUNK), 1)
    causal = col <= row
    for g in range(ngroup):
        lanes = slice(g * CHUNK, (g + 1) * CHUNK)
        wg = jnp.where(causal, w_ref[g], 0.0).astype(_bf16)
        rhs = jnp.concatenate(
            [vn[c * CHUNK:(c + 1) * CHUNK, lanes] for c in range(nchunk)], axis=1)
        vm = jnp.dot(wg, rhs, preferred_element_type=_f32) + bs_ref[g]
        for c in range(nchunk):
            rows = slice(c * CHUNK, (c + 1) * CHUNK)
            o_ref[rows, lanes] = (u_ref[rows, lanes].astype(_f32)
                                  * vm[:, c * CHUNK:(c + 1) * CHUNK]).astype(_bf16)


def _sgu(z, vg, vb, w_s, b_s, bm):
    m, n2 = z.shape
    d = n2 // 2
    ngroup = d // CHUNK
    return pl.pallas_call(
        functools.partial(_sgu_kernel, nchunk=bm // CHUNK, ngroup=ngroup),
        grid=(m // bm,),
        in_specs=[pl.BlockSpec((bm, d), lambda i: (i, 0)),
                  pl.BlockSpec((bm, d), lambda i: (i, 1)),
                  pl.BlockSpec((1, d), lambda i: (0, 0)),
                  pl.BlockSpec((1, d), lambda i: (0, 0)),
                  pl.BlockSpec((ngroup, CHUNK, CHUNK), lambda i: (0, 0, 0)),
                  pl.BlockSpec((ngroup, CHUNK, 1), lambda i: (0, 0, 0))],
        out_specs=pl.BlockSpec((bm, d), lambda i: (i, 0)),
        out_shape=jax.ShapeDtypeStruct((m, d), _bf16),
        compiler_params=_params(("arbitrary",)),
    )(z, z, vg.reshape(1, d), vb.reshape(1, d), w_s, b_s.reshape(ngroup, CHUNK, 1))


def _attn_kernel(sink_ref, q_ref, kp_ref, kc_ref, vp_ref, vc_ref, o_ref, *, gqa):
    n = pl.program_id(1)
    blk = q_ref.shape[0]
    npair = gqa // 2
    pair_w = 2 * HEAD_DIM
    lower_kv = lax.broadcasted_iota(jnp.int32, (2 * blk, pair_w), 1) < HEAD_DIM
    lower_q = lax.broadcasted_iota(jnp.int32, (blk, pair_w), 1) < HEAD_DIM
    qi = lax.broadcasted_iota(jnp.int32, (blk, 2 * blk), 0)
    kj = lax.broadcasted_iota(jnp.int32, (blk, 2 * blk), 1)
    valid = (kj > qi) & (kj <= qi + WINDOW) & ((n > 0) | (kj >= blk))

    def halves(prev_ref, cur_ref, hp, fill):
        lanes = slice(hp * pair_w, (hp + 1) * pair_w)
        both = jnp.concatenate([prev_ref[:, lanes], cur_ref[:, lanes]], axis=0).astype(_f32)
        swapped = pltpu.roll(both, HEAD_DIM, 1)
        pad = jnp.full_like(both, fill)
        head_a = (jnp.where(lower_kv, both, pad), jnp.where(lower_kv, pad, swapped))
        head_b = (jnp.where(lower_kv, swapped, pad), jnp.where(lower_kv, pad, both))
        return [tuple(t.astype(_bf16) for t in head_a), tuple(t.astype(_bf16) for t in head_b)]

    for hp in range(N_KV // 2):
        k_heads = halves(kp_ref, kc_ref, hp, 0.0)
        v_heads = halves(vp_ref, vc_ref, hp, 1.0)
        for sub in range(2):
            h = 2 * hp + sub
            base = h * gqa * HEAD_DIM
            qs = jnp.concatenate(
                [q_ref[:, base + p * pair_w: base + (p + 1) * pair_w] for p in range(npair)], axis=0)
            pv, sink_terms = [], []
            for slot in range(2):
                s_all = lax.dot_general(qs, k_heads[sub][slot], (((1,), (1,)), ((), ())),
                                        preferred_element_type=_f32)
                e_slabs, terms = [], []
                for p in range(npair):
                    s = jnp.where(valid, s_all[p * blk:(p + 1) * blk], -jnp.inf)
                    sink = sink_ref[h * gqa + 2 * p + slot]
                    mx = jnp.maximum(jnp.max(s, axis=-1, keepdims=True), sink)
                    e_slabs.append(jnp.exp(s - mx).astype(_bf16))
                    terms.append(jnp.exp(sink - mx))
                pv.append(jnp.dot(jnp.concatenate(e_slabs, axis=0), v_heads[sub][slot],
                                  preferred_element_type=_f32))
                sink_terms.append(terms)
            for p in range(npair):
                a = pv[0][p * blk:(p + 1) * blk]
                b = pv[1][p * blk:(p + 1) * blk]
                num = jnp.where(lower_q, a, b)
                den = pltpu.roll(jnp.where(lower_q, b, a), HEAD_DIM, 1)
                den = den + jnp.where(lower_q, sink_terms[0][p], sink_terms[1][p])
                o_ref[:, base + p * pair_w: base + (p + 1) * pair_w] = (num / den).astype(o_ref.dtype)


def _attention(q, kv, sinks, batch, seq):
    m, d = q.shape
    blk = WINDOW
    nb = seq // blk
    kvw = N_KV * HEAD_DIM
    gqa = d // (N_KV * HEAD_DIM)

    def cur(col):
        return lambda b, n: (b * nb + n, col)

    def prev(col):
        return lambda b, n: (b * nb + jnp.maximum(n - 1, 0), col)

    return pl.pallas_call(
        functools.partial(_attn_kernel, gqa=gqa),
        grid=(batch, nb),
        in_specs=[pl.BlockSpec(memory_space=pltpu.SMEM),
                  pl.BlockSpec((blk, d), cur(0)),
                  pl.BlockSpec((blk, kvw), prev(0)),
                  pl.BlockSpec((blk, kvw), cur(0)),
                  pl.BlockSpec((blk, kvw), prev(1)),
                  pl.BlockSpec((blk, kvw), cur(1))],
        out_specs=pl.BlockSpec((blk, d), cur(0)),
        out_shape=jax.ShapeDtypeStruct((m, d), _bf16),
        compiler_params=_params(("arbitrary", "arbitrary")),
    )(sinks, q, kv, kv, kv, kv)


def _ffn_kernel(ia_ref, fa_ref, fb_ref, ic_ref, fc_ref, start_ref,
                hb_ref, hf_ref, wg_ref, wu_ref, cp_ref, wd_ref, gb_ref,
                o_ref, ob_ref, zs0_ref, zs1_ref, act0_ref, act1_ref, tail_ref, stat_ref,
                *, alpha, nf, nres):
    t = pl.program_id(0)
    bm = hb_ref.shape[0]
    res_rows = hf_ref.shape[0]
    fa = fa_ref[t]
    seq_start = start_ref[t] == 1
    fc = fc_ref[t]

    @pl.when(t == 0)
    def _():
        zs0_ref[...] = jnp.zeros_like(zs0_ref)
        zs1_ref[...] = jnp.zeros_like(zs1_ref)
        act0_ref[...] = jnp.zeros_like(act0_ref)
        act1_ref[...] = jnp.zeros_like(act1_ref)

    @pl.when(fc == 0)
    def _():
        o_ref[...] = jnp.zeros_like(o_ref)

    bf = wg_ref.shape[1]

    def conv_tile(zs_ref, half, r):
        rows = CONV_TILE_ROWS
        lanes = slice(half * bf, (half + 1) * bf)
        z0 = zs_ref[half, SUBLANES + r:SUBLANES + r + rows, :]
        z1 = zs_ref[half, SUBLANES - 1 + r:SUBLANES - 1 + r + rows, :]
        z2 = zs_ref[half, SUBLANES - 2 + r:SUBLANES - 2 + r + rows, :]
        return (cp_ref[0:1, lanes] * z2 + cp_ref[1:2, lanes] * z1 + cp_ref[2:3, lanes] * z0
                + cp_ref[3:4, lanes])

    def stages(zs_a, zs_b, act_b, act_c):
        row_groups = [slice(r, r + MM_ROWS) for r in range(0, bm, MM_ROWS)]
        for half, w_ref in ((0, wg_ref), (1, wu_ref)):
            zs_a[half, :SUBLANES, :] = jnp.where(seq_start, 0.0, tail_ref[fa, half])
            for rows in row_groups:
                zs_a[half, SUBLANES + rows.start:SUBLANES + rows.stop, :] = jnp.dot(
                    hb_ref[rows, :], w_ref[...], preferred_element_type=_f32)
            tail_ref[fa, half] = zs_a[half, bm:, :]
        nslab = o_ref.shape[1] // DOWN_SLAB_COLS
        tiles_per_slab = bm // CONV_TILE_ROWS // nslab
        for k in range(nslab):
            cols = slice(k * DOWN_SLAB_COLS, (k + 1) * DOWN_SLAB_COLS)
            for rows in row_groups:
                o_ref[rows, cols] += jnp.dot(act_c[rows, :], wd_ref[:, cols],
                                             preferred_element_type=_f32)
            for j in range(tiles_per_slab):
                r = (k * tiles_per_slab + j) * CONV_TILE_ROWS
                gate = conv_tile(zs_b, 0, r)
                up = conv_tile(zs_b, 1, r)
                act_b[r:r + CONV_TILE_ROWS, :] = (gate * jax.nn.sigmoid(gate) * up).astype(_bf16)

    @pl.when(t % 2 == 0)
    def _():
        stages(zs0_ref, zs1_ref, act1_ref, act0_ref)

    @pl.when(t % 2 == 1)
    def _():
        stages(zs1_ref, zs0_ref, act0_ref, act1_ref)

    @pl.when((fc < nres) & (t >= 2))
    def _():
        rows = pl.ds(pl.multiple_of(fc * res_rows, res_rows), res_rows)
        o_ref[rows, :] += alpha * hf_ref[...]

    @pl.when((fc == nf - 1) & (t >= 2))
    def _():
        _layer_norm_rows(o_ref, ob_ref, gb_ref, stat_ref)


def _ffn(hb, hf, w_up, w_down, layer, conv_w, conv_b, g, b, alpha, seq, bm, bf, nres):
    m, d = hb.shape
    d_ff = w_down.shape[1]
    nf = d_ff // bf
    nchunks = (m // bm) * nf
    res_rows = bm // nres
    assert nres <= nf and seq % bm == 0 and bm % CONV_TILE_ROWS == 0 and res_rows % SUBLANES == 0
    taps = jnp.concatenate([conv_w, conv_b[None, :]], axis=0).reshape(CONV_W + 1, 2, nf, bf)
    conv_params = jnp.pad(taps.transpose(2, 0, 1, 3).reshape(nf, CONV_W + 1, 2 * bf),
                          ((0, 0), (0, SUBLANES - CONV_W - 1), (0, 0)))
    gain_bias = jnp.stack([g, b])

    steps = np.arange(nchunks + 2)
    chunk_a = np.minimum(steps, nchunks - 1)
    chunk_b = np.clip(steps - 1, 0, nchunks - 1)
    chunk_c = np.maximum(steps - 2, 0)
    tables = [chunk_a // nf, chunk_a % nf, chunk_b % nf, chunk_c // nf, chunk_c % nf,
              ((chunk_a // nf) % (seq // bm) == 0).astype(np.int32)]
    tables = [jnp.asarray(tbl, jnp.int32) for tbl in tables]

    def res_map(t, ia, fa, fb, ic, fc, start):
        return (ic[t] * nres + jnp.minimum(fc[t], nres - 1), 0)

    grid_spec = pltpu.PrefetchScalarGridSpec(
        num_scalar_prefetch=len(tables),
        grid=(nchunks + 2,),
        in_specs=[pl.BlockSpec((bm, d), lambda t, ia, fa, fb, ic, fc, start: (ia[t], 0),
                               pipeline_mode=pl.Buffered(1)),
                  pl.BlockSpec((res_rows, d), res_map),
                  pl.BlockSpec((None, d, bf), lambda t, ia, fa, fb, ic, fc, start: (layer, 0, fa[t])),
                  pl.BlockSpec((None, d, bf), lambda t, ia, fa, fb, ic, fc, start: (layer, 0, nf + fa[t])),
                  pl.BlockSpec((None, SUBLANES, 2 * bf),
                               lambda t, ia, fa, fb, ic, fc, start: (fb[t], 0, 0)),
                  pl.BlockSpec((None, bf, d), lambda t, ia, fa, fb, ic, fc, start: (layer, fc[t], 0)),
                  pl.BlockSpec((2, d), lambda t, ia, fa, fb, ic, fc, start: (0, 0))],
        out_specs=[pl.BlockSpec((bm, d), lambda t, ia, fa, fb, ic, fc, start: (ic[t], 0),
                                pipeline_mode=pl.Buffered(1)),
                   pl.BlockSpec((bm, d), lambda t, ia, fa, fb, ic, fc, start: (ic[t], 0),
                                pipeline_mode=pl.Buffered(1))],
        scratch_shapes=[pltpu.VMEM((2, SUBLANES + bm, bf), _f32),
                        pltpu.VMEM((2, SUBLANES + bm, bf), _f32),
                        pltpu.VMEM((bm, bf), _bf16),
                        pltpu.VMEM((bm, bf), _bf16),
                        pltpu.VMEM((nf, 2, SUBLANES, bf), _f32),
                        pltpu.VMEM((2, bm, 1), _f32)])
    return pl.pallas_call(
        functools.partial(_ffn_kernel, alpha=alpha, nf=nf, nres=nres),
        grid_spec=grid_spec,
        out_shape=[jax.ShapeDtypeStruct((m, d), _f32),
                   jax.ShapeDtypeStruct((m, d), _bf16)],
        compiler_params=_params(("arbitrary",)),
    )(*tables, hb, hf, w_up, w_up, conv_params, w_down, gain_bias)


def _gelu(z):
    return 0.5 * z * (1.0 + lax.erf(z * (2.0 ** -0.5)))


def _identity(z):
    return z


def kernel(x, mix_in_a, norm_v_a_g, norm_v_a_b, sgu_w, sgu_b, mix_out_a, w_kv, mix_in_b, sinks,
           mix_out_b, ffn_up, ffn_conv_w, ffn_conv_b, ffn_down, ln_g, ln_b):
    batch, seq, d = x.shape
    depth = ffn_up.shape[0]
    n_a = mix_in_a.shape[0]
    alpha = (2.0 * depth) ** 0.25
    scale = HEAD_DIM ** -0.5
    m = batch * seq

    w_in_a = _cast_bf16(mix_in_a, block_rows=256)
    w_in_b = _cast_bf16(mix_in_b, block_rows=512)
    w_kv_b = _cast_bf16(w_kv[None], block_rows=1024)
    w_up = _cast_bf16(ffn_up, block_rows=64)
    w_down = _cast_bf16(ffn_down, block_rows=512)

    out_slab = min(1024, d)

    hf = x.reshape(m, d)
    hb = hf.astype(_bf16)
    kv = None
    for l in range(depth):
        if l < n_a:
            z = _mm_act(hb, w_in_a, l, _gelu, _bf16, bm=1024, bn=1024)
            mix_in = _sgu(z, norm_v_a_g[l], norm_v_a_b[l], sgu_w[l], sgu_b[l], bm=256)
            w_out = _cast_bf16_column_slabs(mix_out_a, l, bn=out_slab, block_rows=512)
        else:
            j = l - n_a
            if kv is None:
                kv = _mm_act(hb, w_kv_b, 0, _identity, _bf16, bm=1024, bn=1024)
            q = _mm_act(hb, w_in_b, j, lambda t: t * scale, _bf16, bm=1024, bn=1024)
            mix_in = _attention(q, kv, sinks[j], batch, seq)
            w_out = _cast_bf16_column_slabs(mix_out_b, j, bn=out_slab, block_rows=512)
        hf, hb = _mm_res_ln(mix_in, w_out, hf, ln_g[l, 0], ln_b[l, 0], alpha, bm=256)
        hf, hb = _ffn(hb, hf, w_up, w_down, l, ffn_conv_w[l], ffn_conv_b[l],
                      ln_g[l, 1], ln_b[l, 1], alpha, seq, bm=min(1024, seq), bf=256,
                      nres=min(16, ffn_down.shape[1] // 256))
    return hf.reshape(batch, seq, d)
```

```python
import functools

import jax
import jax.numpy as jnp
import numpy as np
from jax import lax
from jax.experimental import pallas as pl
from jax.experimental.pallas import tpu as pltpu

LN_EPS = 1e-5
CHUNK = 128
HEAD_DIM = 64
N_KV = 8
WINDOW = 128
CONV_W = 3

VMEM_LIMIT_BYTES = 56 * 1024 * 1024
SUBLANES = 8
LN_ROWS = 16
LN_UNROLL = 8
LN_COLS = 1024
CONV_TILE_ROWS = 32

_f32 = jnp.float32
_bf16 = jnp.bfloat16


def _params(sem):
    return pltpu.CompilerParams(dimension_semantics=sem, vmem_limit_bytes=VMEM_LIMIT_BYTES)


def _layer_norm(y, g, b):
    mu = jnp.mean(y, axis=-1, keepdims=True)
    yc = y - mu
    var = jnp.mean(yc * yc, axis=-1, keepdims=True)
    return yc * lax.rsqrt(var + LN_EPS) * g + b


def _layer_norm_rows(o_ref, ob_ref, gb_ref, stat_ref):
    ngroups = o_ref.shape[0] // LN_ROWS

    def group(r):
        return pl.ds(pl.multiple_of(r * LN_ROWS, LN_ROWS), LN_ROWS)

    def mean_pass(r, carry):
        rows = group(r)
        stat_ref[0, rows, :] = jnp.mean(o_ref[rows, :], axis=-1, keepdims=True)
        return carry

    n = o_ref.shape[1]
    slabs = [slice(c, c + LN_COLS) for c in range(0, n, LN_COLS)]

    def var_pass(r, carry):
        rows = group(r)
        mu = stat_ref[0, rows, :]
        sq = None
        for cols in slabs:
            yc = o_ref[rows, cols] - mu
            sq = yc * yc if sq is None else sq + yc * yc
        ssq = jnp.sum(sq, axis=-1, keepdims=True)
        stat_ref[1, rows, :] = lax.rsqrt(ssq * (1.0 / n) + LN_EPS)
        return carry

    def norm_pass(r, carry):
        rows = group(r)
        mu = stat_ref[0, rows, :]
        inv = stat_ref[1, rows, :]
        for cols in slabs:
            h = (o_ref[rows, cols] - mu) * inv * gb_ref[0:1, cols] + gb_ref[1:2, cols]
            o_ref[rows, cols] = h
            ob_ref[rows, cols] = h.astype(_bf16)
        return carry

    for one_pass in (mean_pass, var_pass, norm_pass):
        lax.fori_loop(0, ngroups, one_pass, 0, unroll=LN_UNROLL)


def _mm_act_kernel(x_ref, w_ref, o_ref, *, act):
    z = jnp.dot(x_ref[...], w_ref[...], preferred_element_type=_f32)
    o_ref[...] = act(z).astype(o_ref.dtype)


def _mm_act(x, w, layer, act, out_dtype, bm, bn):
    m, k = x.shape
    n = w.shape[2]
    return pl.pallas_call(
        functools.partial(_mm_act_kernel, act=act),
        grid=(m // bm, n // bn),
        in_specs=[pl.BlockSpec((bm, k), lambda i, j: (i, 0)),
                  pl.BlockSpec((None, k, bn), lambda i, j: (layer, 0, j))],
        out_specs=pl.BlockSpec((bm, bn), lambda i, j: (i, j)),
        out_shape=jax.ShapeDtypeStruct((m, n), out_dtype),
        compiler_params=_params(("arbitrary", "arbitrary")),
    )(x, w)


def _cast_kernel(x_ref, o_ref):
    o_ref[...] = x_ref[...].astype(o_ref.dtype)


def _cast_bf16(w, block_rows):
    c = w.shape[-1]
    w2 = w.reshape(-1, c)
    r = w2.shape[0]
    assert r % block_rows == 0
    out = pl.pallas_call(
        _cast_kernel,
        grid=(r // block_rows,),
        in_specs=[pl.BlockSpec((block_rows, c), lambda i: (i, 0))],
        out_specs=pl.BlockSpec((block_rows, c), lambda i: (i, 0)),
        out_shape=jax.ShapeDtypeStruct((r, c), _bf16),
        compiler_params=_params(("arbitrary",)),
    )(w2)
    return out.reshape(w.shape)


def _cast_bf16_column_slabs(w, layer, bn, block_rows):
    _, k, n = w.shape
    return pl.pallas_call(
        _cast_kernel,
        grid=(k // block_rows, n // bn),
        in_specs=[pl.BlockSpec((None, block_rows, bn), lambda i, j: (layer, i, j))],
        out_specs=pl.BlockSpec((None, block_rows, bn), lambda i, j: (j, i, 0)),
        out_shape=jax.ShapeDtypeStruct((n // bn, k, bn), _bf16),
        compiler_params=_params(("arbitrary", "arbitrary")),
    )(w)


def _mm_res_ln_kernel(x_ref, w_ref, res_ref, g_ref, b_ref, o_ref, ob_ref, y0_ref, y1_ref,
                      *, alpha, nslab):
    i = pl.program_id(0)
    j = pl.program_id(1)
    bn = w_ref.shape[2]
    rows_out = o_ref.shape[0]

    @pl.when((i == 0) & (j == 0))
    def _():
        y1_ref[...] = jnp.zeros_like(y1_ref)

    def step(y_new, y_done):
        y_new[j] = alpha * res_ref[...] + jnp.dot(x_ref[...], w_ref[j], preferred_element_type=_f32)
        for r in range(0, rows_out, LN_ROWS):
            rows = pl.ds(pl.multiple_of(j * rows_out + r, LN_ROWS), LN_ROWS)
            parts = [y_done[s, rows, :] for s in range(nslab)]
            mu = sum(jnp.sum(p, axis=-1, keepdims=True) for p in parts) * (1.0 / (nslab * bn))
            parts = [p - mu for p in parts]
            var = sum(jnp.sum(p * p, axis=-1, keepdims=True) for p in parts) * (1.0 / (nslab * bn))
            inv = lax.rsqrt(var + LN_EPS)
            for s in range(nslab):
                cols = slice(s * bn, (s + 1) * bn)
                h = parts[s] * inv * g_ref[:, cols] + b_ref[:, cols]
                o_ref[r:r + LN_ROWS, cols] = h
                ob_ref[r:r + LN_ROWS, cols] = h.astype(_bf16)

    @pl.when(i % 2 == 0)
    def _():
        step(y0_ref, y1_ref)

    @pl.when(i % 2 == 1)
    def _():
        step(y1_ref, y0_ref)


def _mm_res_ln(x, w_slabs, res, g, b, alpha, bm):
    m, k = x.shape
    nslab, _, bn = w_slabs.shape
    n = nslab * bn
    nblk = m // bm
    rows_out = bm // nslab
    assert rows_out % LN_ROWS == 0

    def block_in(i):
        return jnp.minimum(i, nblk - 1)

    def out_map(i, j):
        return (jnp.maximum(i - 1, 0) * nslab + j, 0)

    return pl.pallas_call(
        functools.partial(_mm_res_ln_kernel, alpha=alpha, nslab=nslab),
        grid=(nblk + 1, nslab),
        in_specs=[pl.BlockSpec((bm, k), lambda i, j: (block_in(i), 0)),
                  pl.BlockSpec((nslab, k, bn), lambda i, j: (0, 0, 0), pipeline_mode=pl.Buffered(1)),
                  pl.BlockSpec((bm, bn), lambda i, j: (block_in(i), j)),
                  pl.BlockSpec((1, n), lambda i, j: (0, 0)),
                  pl.BlockSpec((1, n), lambda i, j: (0, 0))],
        out_specs=[pl.BlockSpec((rows_out, n), out_map),
                   pl.BlockSpec((rows_out, n), out_map)],
        out_shape=[jax.ShapeDtypeStruct((m, n), _f32),
                   jax.ShapeDtypeStruct((m, n), _bf16)],
        scratch_shapes=[pltpu.VMEM((nslab, bm, bn), _f32),
                        pltpu.VMEM((nslab, bm, bn), _f32)],
        compiler_params=_params(("arbitrary", "arbitrary")),
    )(x, w_slabs, res, g.reshape(1, n), b.reshape(1, n))


def _sgu_kernel(u_ref, v_ref, vg_ref, vb_ref, w_ref, bs_ref, o_ref, *, nchunk, ngroup):
    vn = _layer_norm(v_ref[...].astype(_f32), vg_ref[...], vb_ref[...]).astype(_bf16)
    row = lax.broadcasted_iota(jnp.int32, (CHUNK, CHUNK), 0)
    col = lax.broadcasted_iota(jnp.int32, (CHUNK, CHUNK), 1)
    causal = col <= row
    for g in range(ngroup):
        lanes = slice(g * CHUNK, (g + 1) * CHUNK)
        wg = jnp.where(causal, w_ref[g], 0.0).astype(_bf16)
        rhs = jnp.concatenate(
            [vn[c * CHUNK:(c + 1) * CHUNK, lanes] for c in range(nchunk)], axis=1)
        vm = jnp.dot(wg, rhs, preferred_element_type=_f32) + bs_ref[g]
        for c in range(nchunk):
            rows = slice(c * CHUNK, (c + 1) * CHUNK)
            o_ref[rows, lanes] = (u_ref[rows, lanes].astype(_f32)
                                  * vm[:, c * CHUNK:(c + 1) * CHUNK]).astype(_bf16)


def _sgu(z, vg, vb, w_s, b_s, bm):
    m, n2 = z.shape
    d = n2 // 2
    ngroup = d // CHUNK
    return pl.pallas_call(
        functools.partial(_sgu_kernel, nchunk=bm // CHUNK, ngroup=ngroup),
        grid=(m // bm,),
        in_specs=[pl.BlockSpec((bm, d), lambda i: (i, 0)),
                  pl.BlockSpec((bm, d), lambda i: (i, 1)),
                  pl.BlockSpec((1, d), lambda i: (0, 0)),
                  pl.BlockSpec((1, d), lambda i: (0, 0)),
                  pl.BlockSpec((ngroup, CHUNK, CHUNK), lambda i: (0, 0, 0)),
                  pl.BlockSpec((ngroup, CHUNK, 1), lambda i: (0, 0, 0))],
        out_specs=pl.BlockSpec((bm, d), lambda i: (i, 0)),
        out_shape=jax.ShapeDtypeStruct((m, d), _bf16),
        compiler_params=_params(("arbitrary",)),
    )(z, z, vg.reshape(1, d), vb.reshape(1, d), w_s, b_s.reshape(ngroup, CHUNK, 1))


def _attn_kernel(sink_ref, q_ref, kp_ref, kc_ref, vp_ref, vc_ref, o_ref, *, gqa):
    n = pl.program_id(1)
    blk = q_ref.shape[0]
    npair = gqa // 2
    pair_w = 2 * HEAD_DIM
    lower_kv = lax.broadcasted_iota(jnp.int32, (2 * blk, pair_w), 1) < HEAD_DIM
    key0 = lax.broadcasted_iota(jnp.int32, (2 * blk, pair_w), 0) == 0
    lower_q = lax.broadcasted_iota(jnp.int32, (blk, pair_w), 1) < HEAD_DIM
    qi = lax.broadcasted_iota(jnp.int32, (blk, 2 * blk), 0)
    kj = lax.broadcasted_iota(jnp.int32, (blk, 2 * blk), 1)
    valid = (kj > qi) & (kj <= qi + WINDOW) & ((n > 0) | (kj >= blk))
    bias = jnp.where(valid, 0.0, -jnp.inf)
    sink_col = kj == 0

    def halves(prev_ref, cur_ref, hp, fill):
        lanes = slice(hp * pair_w, (hp + 1) * pair_w)
        both = jnp.concatenate([prev_ref[:, lanes], cur_ref[:, lanes]], axis=0).astype(_f32)
        both = jnp.where(key0, 0.0, both)
        swapped = pltpu.roll(both, HEAD_DIM, 1)
        pad = jnp.full_like(both, fill)
        head_a = (jnp.where(lower_kv, both, pad), jnp.where(lower_kv, pad, swapped))
        head_b = (jnp.where(lower_kv, swapped, pad), jnp.where(lower_kv, pad, both))
        return [tuple(t.astype(_bf16) for t in head_a), tuple(t.astype(_bf16) for t in head_b)]

    for hp in range(N_KV // 2):
        k_heads = halves(kp_ref, kc_ref, hp, 0.0)
        v_heads = halves(vp_ref, vc_ref, hp, 1.0)
        for sub in range(2):
            h = 2 * hp + sub
            base = h * gqa * HEAD_DIM
            qs = jnp.concatenate(
                [q_ref[:, base + p * pair_w: base + (p + 1) * pair_w] for p in range(npair)], axis=0)
            pv = []
            for slot in range(2):
                s_all = lax.dot_general(qs, k_heads[sub][slot], (((1,), (1,)), ((), ())),
                                        preferred_element_type=_f32)
                e_slabs = []
                for p in range(npair):
                    sink = sink_ref[h * gqa + 2 * p + slot]
                    s = s_all[p * blk:(p + 1) * blk] + jnp.where(sink_col, sink, bias)
                    mx = jnp.max(s, axis=-1, keepdims=True)
                    e_slabs.append(jnp.exp(s - mx).astype(_bf16))
                pv.append(jnp.dot(jnp.concatenate(e_slabs, axis=0), v_heads[sub][slot],
                                  preferred_element_type=_f32))
            for p in range(npair):
                a = pv[0][p * blk:(p + 1) * blk]
                b = pv[1][p * blk:(p + 1) * blk]
                num = jnp.where(lower_q, a, b)
                den = pltpu.roll(jnp.where(lower_q, b, a), HEAD_DIM, 1)
                o_ref[:, base + p * pair_w: base + (p + 1) * pair_w] = (num / den).astype(o_ref.dtype)


def _attention(q, kv, sinks, batch, seq):
    m, d = q.shape
    blk = WINDOW
    nb = seq // blk
    kvw = N_KV * HEAD_DIM
    gqa = d // (N_KV * HEAD_DIM)

    def cur(col):
        return lambda b, n: (b * nb + n, col)

    def prev(col):
        return lambda b, n: (b * nb + jnp.maximum(n - 1, 0), col)

    return pl.pallas_call(
        functools.partial(_attn_kernel, gqa=gqa),
        grid=(batch, nb),
        in_specs=[pl.BlockSpec(memory_space=pltpu.SMEM),
                  pl.BlockSpec((blk, d), cur(0)),
                  pl.BlockSpec((blk, kvw), prev(0)),
                  pl.BlockSpec((blk, kvw), cur(0)),
                  pl.BlockSpec((blk, kvw), prev(1)),
                  pl.BlockSpec((blk, kvw), cur(1))],
        out_specs=pl.BlockSpec((blk, d), cur(0)),
        out_shape=jax.ShapeDtypeStruct((m, d), _bf16),
        compiler_params=_params(("arbitrary", "arbitrary")),
    )(sinks, q, kv, kv, kv, kv)


def _ffn_kernel(ia_ref, fa_ref, fb_ref, ic_ref, fc_ref, start_ref,
                hb_ref, hf_ref, wg_ref, wu_ref, cp_ref, wd_ref, gb_ref,
                o_ref, ob_ref, zs0_ref, zs1_ref, act0_ref, act1_ref, tail_ref, stat_ref,
                *, alpha, nf, nres):
    t = pl.program_id(0)
    bm = hb_ref.shape[0]
    res_rows = hf_ref.shape[0]
    fa = fa_ref[t]
    seq_start = start_ref[t] == 1
    fc = fc_ref[t]

    @pl.when(t == 0)
    def _():
        zs0_ref[...] = jnp.zeros_like(zs0_ref)
        zs1_ref[...] = jnp.zeros_like(zs1_ref)
        act0_ref[...] = jnp.zeros_like(act0_ref)
        act1_ref[...] = jnp.zeros_like(act1_ref)

    @pl.when(fc == 0)
    def _():
        o_ref[...] = jnp.zeros_like(o_ref)

    bf = wg_ref.shape[1]

    def conv_tile(zs_ref, half, r):
        rows = CONV_TILE_ROWS
        lanes = slice(half * bf, (half + 1) * bf)
        z0 = zs_ref[half, SUBLANES + r:SUBLANES + r + rows, :]
        z1 = zs_ref[half, SUBLANES - 1 + r:SUBLANES - 1 + r + rows, :]
        z2 = zs_ref[half, SUBLANES - 2 + r:SUBLANES - 2 + r + rows, :]
        return (cp_ref[0:1, lanes] * z2 + cp_ref[1:2, lanes] * z1 + cp_ref[2:3, lanes] * z0
                + cp_ref[3:4, lanes])

    def stages(zs_a, zs_b, act_b, act_c):
        hb = hb_ref[...]
        for half, w_ref in ((0, wg_ref), (1, wu_ref)):
            zs_a[half, :SUBLANES, :] = jnp.where(seq_start, 0.0, tail_ref[fa, half])
            zs_a[half, SUBLANES:, :] = jnp.dot(hb, w_ref[...], preferred_element_type=_f32)
            tail_ref[fa, half] = zs_a[half, bm:, :]
        ntile = bm // CONV_TILE_ROWS
        bn = o_ref.shape[1] // ntile
        for k in range(ntile):
            cols = slice(k * bn, (k + 1) * bn)
            o_ref[:, cols] += jnp.dot(act_c[...], wd_ref[:, cols], preferred_element_type=_f32)
            r = k * CONV_TILE_ROWS
            gate = conv_tile(zs_b, 0, r)
            up = conv_tile(zs_b, 1, r)
            act_b[r:r + CONV_TILE_ROWS, :] = (gate * jax.nn.sigmoid(gate) * up).astype(_bf16)

    @pl.when(t % 2 == 0)
    def _():
        stages(zs0_ref, zs1_ref, act1_ref, act0_ref)

    @pl.when(t % 2 == 1)
    def _():
        stages(zs1_ref, zs0_ref, act0_ref, act1_ref)

    @pl.when((fc < nres) & (t >= 2))
    def _():
        rows = pl.ds(pl.multiple_of(fc * res_rows, res_rows), res_rows)
        o_ref[rows, :] += alpha * hf_ref[...]

    @pl.when((fc == nf - 1) & (t >= 2))
    def _():
        _layer_norm_rows(o_ref, ob_ref, gb_ref, stat_ref)


def _ffn(hb, hf, w_up, w_down, layer, conv_w, conv_b, g, b, alpha, seq, bm, bf, nres):
    m, d = hb.shape
    d_ff = w_down.shape[1]
    nf = d_ff // bf
    nchunks = (m // bm) * nf
    res_rows = bm // nres
    assert nres <= nf and seq % bm == 0 and bm % CONV_TILE_ROWS == 0 and res_rows % SUBLANES == 0
    taps = jnp.concatenate([conv_w, conv_b[None, :]], axis=0).reshape(CONV_W + 1, 2, nf, bf)
    conv_params = jnp.pad(taps.transpose(2, 0, 1, 3).reshape(nf, CONV_W + 1, 2 * bf),
                          ((0, 0), (0, SUBLANES - CONV_W - 1), (0, 0)))
    gain_bias = jnp.stack([g, b])

    steps = np.arange(nchunks + 2)
    chunk_a = np.minimum(steps, nchunks - 1)
    chunk_b = np.clip(steps - 1, 0, nchunks - 1)
    chunk_c = np.maximum(steps - 2, 0)
    tables = [chunk_a // nf, chunk_a % nf, chunk_b % nf, chunk_c // nf, chunk_c % nf,
              ((chunk_a // nf) % (seq // bm) == 0).astype(np.int32)]
    tables = [jnp.asarray(tbl, jnp.int32) for tbl in tables]

    def res_map(t, ia, fa, fb, ic, fc, start):
        return (ic[t] * nres + jnp.minimum(fc[t], nres - 1), 0)

    grid_spec = pltpu.PrefetchScalarGridSpec(
        num_scalar_prefetch=len(tables),
        grid=(nchunks + 2,),
        in_specs=[pl.BlockSpec((bm, d), lambda t, ia, fa, fb, ic, fc, start: (ia[t], 0)),
                  pl.BlockSpec((res_rows, d), res_map),
                  pl.BlockSpec((None, d, bf), lambda t, ia, fa, fb, ic, fc, start: (layer, 0, fa[t])),
                  pl.BlockSpec((None, d, bf), lambda t, ia, fa, fb, ic, fc, start: (layer, 0, nf + fa[t])),
                  pl.BlockSpec((None, SUBLANES, 2 * bf),
                               lambda t, ia, fa, fb, ic, fc, start: (fb[t], 0, 0)),
                  pl.BlockSpec((None, bf, d), lambda t, ia, fa, fb, ic, fc, start: (layer, fc[t], 0)),
                  pl.BlockSpec((2, d), lambda t, ia, fa, fb, ic, fc, start: (0, 0))],
        out_specs=[pl.BlockSpec((bm, d), lambda t, ia, fa, fb, ic, fc, start: (ic[t], 0)),
                   pl.BlockSpec((bm, d), lambda t, ia, fa, fb, ic, fc, start: (ic[t], 0))],
        scratch_shapes=[pltpu.VMEM((2, SUBLANES + bm, bf), _f32),
                        pltpu.VMEM((2, SUBLANES + bm, bf), _f32),
                        pltpu.VMEM((bm, bf), _bf16),
                        pltpu.VMEM((bm, bf), _bf16),
                        pltpu.VMEM((nf, 2, SUBLANES, bf), _f32),
                        pltpu.VMEM((2, bm, 1), _f32)])
    return pl.pallas_call(
        functools.partial(_ffn_kernel, alpha=alpha, nf=nf, nres=nres),
        grid_spec=grid_spec,
        out_shape=[jax.ShapeDtypeStruct((m, d), _f32),
                   jax.ShapeDtypeStruct((m, d), _bf16)],
        compiler_params=_params(("arbitrary",)),
    )(*tables, hb, hf, w_up, w_up, conv_params, w_down, gain_bias)


def _gelu(z):
    return 0.5 * z * (1.0 + lax.erf(z * (2.0 ** -0.5)))


def _identity(z):
    return z


def kernel(x, mix_in_a, norm_v_a_g, norm_v_a_b, sgu_w, sgu_b, mix_out_a, w_kv, mix_in_b, sinks,
           mix_out_b, ffn_up, ffn_conv_w, ffn_conv_b, ffn_down, ln_g, ln_b):
    batch, seq, d = x.shape
    depth = ffn_up.shape[0]
    n_a = mix_in_a.shape[0]
    alpha = (2.0 * depth) ** 0.25
    scale = HEAD_DIM ** -0.5
    m = batch * seq

    w_in_a = _cast_bf16(mix_in_a, block_rows=256)
    w_in_b = _cast_bf16(mix_in_b, block_rows=512)
    w_kv_b = _cast_bf16(w_kv[None], block_rows=1024)
    w_up = _cast_bf16(ffn_up, block_rows=64)
    w_down = _cast_bf16(ffn_down, block_rows=512)

    out_slab = min(1024, d)

    hf = x.reshape(m, d)
    hb = hf.astype(_bf16)
    kv = None
    for l in range(depth):
        if l < n_a:
            z = _mm_act(hb, w_in_a, l, _gelu, _bf16, bm=1024, bn=1024)
            mix_in = _sgu(z, norm_v_a_g[l], norm_v_a_b[l], sgu_w[l], sgu_b[l], bm=256)
            w_out = _cast_bf16_column_slabs(mix_out_a, l, bn=out_slab, block_rows=512)
        else:
            j = l - n_a
            if kv is None:
                kv = _mm_act(hb, w_kv_b, 0, _identity, _bf16, bm=1024, bn=1024)
            q = _mm_act(hb, w_in_b, j, lambda t: t * scale, _bf16, bm=1024, bn=1024)
            mix_in = _attention(q, kv, sinks[j], batch, seq)
            w_out = _cast_bf16_column_slabs(mix_out_b, j, bn=out_slab, block_rows=512)
        hf, hb = _mm_res_ln(mix_in, w_out, hf, ln_g[l, 0], ln_b[l, 0], alpha, bm=256)
        hf, hb = _ffn(hb, hf, w_up, w_down, l, ffn_conv_w[l], ffn_conv_b[l],
                      ln_g[l, 1], ln_b[l, 1], alpha, seq, bm=512, bf=256,
                      nres=min(8, ffn_down.shape[1] // 256))
    return hf.reshape(batch, seq, d)
```

```python
import functools

import jax
import jax.numpy as jnp
import numpy as np
from jax import lax
from jax.experimental import pallas as pl
from jax.experimental.pallas import tpu as pltpu

LN_EPS = 1e-5
CHUNK = 128
HEAD_DIM = 64
N_KV = 8
WINDOW = 128
CONV_W = 3

VMEM_LIMIT_BYTES = 56 * 1024 * 1024
SUBLANES = 8
LN_ROWS = 16
LN_UNROLL = 8
LN_COLS = 1024
CONV_TILE_ROWS = 32

_f32 = jnp.float32
_bf16 = jnp.bfloat16


def _params(sem):
    return pltpu.CompilerParams(dimension_semantics=sem, vmem_limit_bytes=VMEM_LIMIT_BYTES)


def _layer_norm(y, g, b):
    mu = jnp.mean(y, axis=-1, keepdims=True)
    yc = y - mu
    var = jnp.mean(yc * yc, axis=-1, keepdims=True)
    return yc * lax.rsqrt(var + LN_EPS) * g + b


def _layer_norm_rows(o_ref, ob_ref, gb_ref, stat_ref):
    ngroups = o_ref.shape[0] // LN_ROWS

    def group(r):
        return pl.ds(pl.multiple_of(r * LN_ROWS, LN_ROWS), LN_ROWS)

    def mean_pass(r, carry):
        rows = group(r)
        stat_ref[0, rows, :] = jnp.mean(o_ref[rows, :], axis=-1, keepdims=True)
        return carry

    n = o_ref.shape[1]
    slabs = [slice(c, c + LN_COLS) for c in range(0, n, LN_COLS)]

    def var_pass(r, carry):
        rows = group(r)
        mu = stat_ref[0, rows, :]
        sq = None
        for cols in slabs:
            yc = o_ref[rows, cols] - mu
            sq = yc * yc if sq is None else sq + yc * yc
        ssq = jnp.sum(sq, axis=-1, keepdims=True)
        stat_ref[1, rows, :] = lax.rsqrt(ssq * (1.0 / n) + LN_EPS)
        return carry

    def norm_pass(r, carry):
        rows = group(r)
        mu = stat_ref[0, rows, :]
        inv = stat_ref[1, rows, :]
        for cols in slabs:
            h = (o_ref[rows, cols] - mu) * inv * gb_ref[0:1, cols] + gb_ref[1:2, cols]
            o_ref[rows, cols] = h
            ob_ref[rows, cols] = h.astype(_bf16)
        return carry

    for one_pass in (mean_pass, var_pass, norm_pass):
        lax.fori_loop(0, ngroups, one_pass, 0, unroll=LN_UNROLL)


def _mm_act_kernel(x_ref, w_ref, o_ref, *, act):
    z = jnp.dot(x_ref[...], w_ref[...], preferred_element_type=_f32)
    o_ref[...] = act(z).astype(o_ref.dtype)


def _mm_act(x, w, layer, act, out_dtype, bm, bn):
    m, k = x.shape
    n = w.shape[2]
    return pl.pallas_call(
        functools.partial(_mm_act_kernel, act=act),
        grid=(m // bm, n // bn),
        in_specs=[pl.BlockSpec((bm, k), lambda i, j: (i, 0)),
                  pl.BlockSpec((None, k, bn), lambda i, j: (layer, 0, j))],
        out_specs=pl.BlockSpec((bm, bn), lambda i, j: (i, j)),
        out_shape=jax.ShapeDtypeStruct((m, n), out_dtype),
        compiler_params=_params(("arbitrary", "arbitrary")),
    )(x, w)


def _cast_kernel(x_ref, o_ref):
    o_ref[...] = x_ref[...].astype(o_ref.dtype)


def _cast_bf16(w, block_rows):
    c = w.shape[-1]
    w2 = w.reshape(-1, c)
    r = w2.shape[0]
    assert r % block_rows == 0
    out = pl.pallas_call(
        _cast_kernel,
        grid=(r // block_rows,),
        in_specs=[pl.BlockSpec((block_rows, c), lambda i: (i, 0))],
        out_specs=pl.BlockSpec((block_rows, c), lambda i: (i, 0)),
        out_shape=jax.ShapeDtypeStruct((r, c), _bf16),
        compiler_params=_params(("arbitrary",)),
    )(w2)
    return out.reshape(w.shape)


def _cast_bf16_layer(w, layer, block_rows):
    _, r, c = w.shape
    assert r % block_rows == 0
    return pl.pallas_call(
        _cast_kernel,
        grid=(r // block_rows,),
        in_specs=[pl.BlockSpec((None, block_rows, c), lambda i: (layer, i, 0))],
        out_specs=pl.BlockSpec((block_rows, c), lambda i: (i, 0)),
        out_shape=jax.ShapeDtypeStruct((r, c), _bf16),
        compiler_params=_params(("arbitrary",)),
    )(w)


def _cast_bf16_column_slabs(w, layer, bn, block_rows):
    _, k, n = w.shape
    return pl.pallas_call(
        _cast_kernel,
        grid=(k // block_rows, n // bn),
        in_specs=[pl.BlockSpec((None, block_rows, bn), lambda i, j: (layer, i, j))],
        out_specs=pl.BlockSpec((None, block_rows, bn), lambda i, j: (j, i, 0)),
        out_shape=jax.ShapeDtypeStruct((n // bn, k, bn), _bf16),
        compiler_params=_params(("arbitrary", "arbitrary")),
    )(w)


def _mm_res_ln_kernel(x_ref, w_ref, res_ref, g_ref, b_ref, o_ref, ob_ref, y0_ref, y1_ref,
                      *, alpha, nslab):
    i = pl.program_id(0)
    j = pl.program_id(1)
    bn = w_ref.shape[2]
    rows_out = o_ref.shape[0]

    @pl.when((i == 0) & (j == 0))
    def _():
        y1_ref[...] = jnp.zeros_like(y1_ref)

    def step(y_new, y_done):
        y_new[j] = alpha * res_ref[...] + jnp.dot(x_ref[...], w_ref[j], preferred_element_type=_f32)
        for r in range(0, rows_out, LN_ROWS):
            rows = pl.ds(pl.multiple_of(j * rows_out + r, LN_ROWS), LN_ROWS)
            parts = [y_done[s, rows, :] for s in range(nslab)]
            mu = sum(jnp.sum(p, axis=-1, keepdims=True) for p in parts) * (1.0 / (nslab * bn))
            parts = [p - mu for p in parts]
            var = sum(jnp.sum(p * p, axis=-1, keepdims=True) for p in parts) * (1.0 / (nslab * bn))
            inv = lax.rsqrt(var + LN_EPS)
            for s in range(nslab):
                cols = slice(s * bn, (s + 1) * bn)
                h = parts[s] * inv * g_ref[:, cols] + b_ref[:, cols]
                o_ref[r:r + LN_ROWS, cols] = h
                ob_ref[r:r + LN_ROWS, cols] = h.astype(_bf16)

    @pl.when(i % 2 == 0)
    def _():
        step(y0_ref, y1_ref)

    @pl.when(i % 2 == 1)
    def _():
        step(y1_ref, y0_ref)


def _mm_res_ln(x, w_slabs, res, g, b, alpha, bm):
    m, k = x.shape
    nslab, _, bn = w_slabs.shape
    n = nslab * bn
    nblk = m // bm
    rows_out = bm // nslab
    assert rows_out % LN_ROWS == 0

    def block_in(i):
        return jnp.minimum(i, nblk - 1)

    def out_map(i, j):
        return (jnp.maximum(i - 1, 0) * nslab + j, 0)

    return pl.pallas_call(
        functools.partial(_mm_res_ln_kernel, alpha=alpha, nslab=nslab),
        grid=(nblk + 1, nslab),
        in_specs=[pl.BlockSpec((bm, k), lambda i, j: (block_in(i), 0)),
                  pl.BlockSpec((nslab, k, bn), lambda i, j: (0, 0, 0), pipeline_mode=pl.Buffered(1)),
                  pl.BlockSpec((bm, bn), lambda i, j: (block_in(i), j)),
                  pl.BlockSpec((1, n), lambda i, j: (0, 0)),
                  pl.BlockSpec((1, n), lambda i, j: (0, 0))],
        out_specs=[pl.BlockSpec((rows_out, n), out_map),
                   pl.BlockSpec((rows_out, n), out_map)],
        out_shape=[jax.ShapeDtypeStruct((m, n), _f32),
                   jax.ShapeDtypeStruct((m, n), _bf16)],
        scratch_shapes=[pltpu.VMEM((nslab, bm, bn), _f32),
                        pltpu.VMEM((nslab, bm, bn), _f32)],
        compiler_params=_params(("arbitrary", "arbitrary")),
    )(x, w_slabs, res, g.reshape(1, n), b.reshape(1, n))


def _sgu_kernel(u_ref, v_ref, vg_ref, vb_ref, w_ref, bs_ref, o_ref, *, nchunk, ngroup):
    vn = _layer_norm(v_ref[...].astype(_f32), vg_ref[...], vb_ref[...]).astype(_bf16)
    row = lax.broadcasted_iota(jnp.int32, (CHUNK, CHUNK), 0)
    col = lax.broadcasted_iota(jnp.int32, (CHUNK, CHUNK), 1)
    causal = col <= row
    for g in range(ngroup):
        lanes = slice(g * CHUNK, (g + 1) * CHUNK)
        wg = jnp.where(causal, w_ref[g], 0.0).astype(_bf16)
        rhs = jnp.concatenate(
            [vn[c * CHUNK:(c + 1) * CHUNK, lanes] for c in range(nchunk)], axis=1)
        vm = jnp.dot(wg, rhs, preferred_element_type=_f32) + bs_ref[g]
        for c in range(nchunk):
            rows = slice(c * CHUNK, (c + 1) * CHUNK)
            o_ref[rows, lanes] = (u_ref[rows, lanes].astype(_f32)
                                  * vm[:, c * CHUNK:(c + 1) * CHUNK]).astype(_bf16)


def _sgu(z, vg, vb, w_s, b_s, bm):
    m, n2 = z.shape
    d = n2 // 2
    ngroup = d // CHUNK
    return pl.pallas_call(
        functools.partial(_sgu_kernel, nchunk=bm // CHUNK, ngroup=ngroup),
        grid=(m // bm,),
        in_specs=[pl.BlockSpec((bm, d), lambda i: (i, 0)),
                  pl.BlockSpec((bm, d), lambda i: (i, 1)),
                  pl.BlockSpec((1, d), lambda i: (0, 0)),
                  pl.BlockSpec((1, d), lambda i: (0, 0)),
                  pl.BlockSpec((ngroup, CHUNK, CHUNK), lambda i: (0, 0, 0)),
                  pl.BlockSpec((ngroup, CHUNK, 1), lambda i: (0, 0, 0))],
        out_specs=pl.BlockSpec((bm, d), lambda i: (i, 0)),
        out_shape=jax.ShapeDtypeStruct((m, d), _bf16),
        compiler_params=_params(("arbitrary",)),
    )(z, z, vg.reshape(1, d), vb.reshape(1, d), w_s, b_s.reshape(ngroup, CHUNK, 1))


def _attn_kernel(sink_ref, q_ref, kp_ref, kc_ref, vp_ref, vc_ref, o_ref, *, gqa):
    n = pl.program_id(1)
    blk = q_ref.shape[0]
    npair = gqa // 2
    pair_w = 2 * HEAD_DIM
    lower_kv = lax.broadcasted_iota(jnp.int32, (2 * blk, pair_w), 1) < HEAD_DIM
    key0 = lax.broadcasted_iota(jnp.int32, (2 * blk, pair_w), 0) == 0
    lower_q = lax.broadcasted_iota(jnp.int32, (blk, pair_w), 1) < HEAD_DIM
    qi = lax.broadcasted_iota(jnp.int32, (blk, 2 * blk), 0)
    kj = lax.broadcasted_iota(jnp.int32, (blk, 2 * blk), 1)
    valid = (kj > qi) & (kj <= qi + WINDOW) & ((n > 0) | (kj >= blk))
    bias = jnp.where(valid, 0.0, -jnp.inf)
    sink_col = kj == 0

    def halves(prev_ref, cur_ref, hp, fill):
        lanes = slice(hp * pair_w, (hp + 1) * pair_w)
        both = jnp.concatenate([prev_ref[:, lanes], cur_ref[:, lanes]], axis=0).astype(_f32)
        both = jnp.where(key0, 0.0, both)
        swapped = pltpu.roll(both, HEAD_DIM, 1)
        pad = jnp.full_like(both, fill)
        head_a = (jnp.where(lower_kv, both, pad), jnp.where(lower_kv, pad, swapped))
        head_b = (jnp.where(lower_kv, swapped, pad), jnp.where(lower_kv, pad, both))
        return [tuple(t.astype(_bf16) for t in head_a), tuple(t.astype(_bf16) for t in head_b)]

    for hp in range(N_KV // 2):
        k_heads = halves(kp_ref, kc_ref, hp, 0.0)
        v_heads = halves(vp_ref, vc_ref, hp, 1.0)
        for sub in range(2):
            h = 2 * hp + sub
            base = h * gqa * HEAD_DIM
            qs = jnp.concatenate(
                [q_ref[:, base + p * pair_w: base + (p + 1) * pair_w] for p in range(npair)], axis=0)
            pv = []
            for slot in range(2):
                s_all = lax.dot_general(qs, k_heads[sub][slot], (((1,), (1,)), ((), ())),
                                        preferred_element_type=_f32)
                e_slabs = []
                for p in range(npair):
                    sink = sink_ref[h * gqa + 2 * p + slot]
                    s = s_all[p * blk:(p + 1) * blk] + jnp.where(sink_col, sink, bias)
                    mx = jnp.max(s, axis=-1, keepdims=True)
                    e_slabs.append(jnp.exp(s - mx).astype(_bf16))
                pv.append(jnp.dot(jnp.concatenate(e_slabs, axis=0), v_heads[sub][slot],
                                  preferred_element_type=_f32))
            for p in range(npair):
                a = pv[0][p * blk:(p + 1) * blk]
                b = pv[1][p * blk:(p + 1) * blk]
                num = jnp.where(lower_q, a, b)
                den = pltpu.roll(jnp.where(lower_q, b, a), HEAD_DIM, 1)
                o_ref[:, base + p * pair_w: base + (p + 1) * pair_w] = (num / den).astype(o_ref.dtype)


def _attention(q, kv, sinks, batch, seq):
    m, d = q.shape
    blk = WINDOW
    nb = seq // blk
    kvw = N_KV * HEAD_DIM
    gqa = d // (N_KV * HEAD_DIM)

    def cur(col):
        return lambda b, n: (b * nb + n, col)

    def prev(col):
        return lambda b, n: (b * nb + jnp.maximum(n - 1, 0), col)

    return pl.pallas_call(
        functools.partial(_attn_kernel, gqa=gqa),
        grid=(batch, nb),
        in_specs=[pl.BlockSpec(memory_space=pltpu.SMEM),
                  pl.BlockSpec((blk, d), cur(0)),
                  pl.BlockSpec((blk, kvw), prev(0)),
                  pl.BlockSpec((blk, kvw), cur(0)),
                  pl.BlockSpec((blk, kvw), prev(1)),
                  pl.BlockSpec((blk, kvw), cur(1))],
        out_specs=pl.BlockSpec((blk, d), cur(0)),
        out_shape=jax.ShapeDtypeStruct((m, d), _bf16),
        compiler_params=_params(("arbitrary", "arbitrary")),
    )(sinks, q, kv, kv, kv, kv)


def _ffn_kernel(ia_ref, fa_ref, fb_ref, ic_ref, fc_ref, start_ref,
                hb_ref, hf_ref, wg_ref, wu_ref, cp_ref, wd_ref, gb_ref, *refs,
                alpha, nf, nres, cast_next):
    if cast_next:
        up_next_ref, down_next_ref = refs[:2]
        o_ref, ob_ref, up_cast_ref, down_cast_ref = refs[2:6]
        zs0_ref, zs1_ref, act0_ref, act1_ref, tail_ref, stat_ref = refs[6:]
    else:
        o_ref, ob_ref = refs[:2]
        zs0_ref, zs1_ref, act0_ref, act1_ref, tail_ref, stat_ref = refs[2:]
    t = pl.program_id(0)
    bm = hb_ref.shape[0]
    res_rows = hf_ref.shape[0]
    fa = fa_ref[t]
    seq_start = start_ref[t] == 1
    fc = fc_ref[t]

    @pl.when(t == 0)
    def _():
        zs0_ref[...] = jnp.zeros_like(zs0_ref)
        zs1_ref[...] = jnp.zeros_like(zs1_ref)
        act0_ref[...] = jnp.zeros_like(act0_ref)
        act1_ref[...] = jnp.zeros_like(act1_ref)

    @pl.when(fc == 0)
    def _():
        o_ref[...] = jnp.zeros_like(o_ref)

    bf = wg_ref.shape[1]

    def conv_tile(zs_ref, half, r):
        rows = CONV_TILE_ROWS
        lanes = slice(half * bf, (half + 1) * bf)
        z0 = zs_ref[half, SUBLANES + r:SUBLANES + r + rows, :]
        z1 = zs_ref[half, SUBLANES - 1 + r:SUBLANES - 1 + r + rows, :]
        z2 = zs_ref[half, SUBLANES - 2 + r:SUBLANES - 2 + r + rows, :]
        return (cp_ref[0:1, lanes] * z2 + cp_ref[1:2, lanes] * z1 + cp_ref[2:3, lanes] * z0
                + cp_ref[3:4, lanes])

    def stages(zs_a, zs_b, act_b, act_c):
        hb = hb_ref[...]
        for half, w_ref in ((0, wg_ref), (1, wu_ref)):
            zs_a[half, :SUBLANES, :] = jnp.where(seq_start, 0.0, tail_ref[fa, half])
            zs_a[half, SUBLANES:, :] = jnp.dot(hb, w_ref[...], preferred_element_type=_f32)
            tail_ref[fa, half] = zs_a[half, bm:, :]
        ntile = bm // CONV_TILE_ROWS
        bn = o_ref.shape[1] // ntile
        for k in range(ntile):
            cols = slice(k * bn, (k + 1) * bn)
            o_ref[:, cols] += jnp.dot(act_c[...], wd_ref[:, cols], preferred_element_type=_f32)
            r = k * CONV_TILE_ROWS
            gate = conv_tile(zs_b, 0, r)
            up = conv_tile(zs_b, 1, r)
            act_b[r:r + CONV_TILE_ROWS, :] = (gate * jax.nn.sigmoid(gate) * up).astype(_bf16)
        if cast_next:
            up_cast_ref[...] = up_next_ref[...].astype(_bf16)
            down_cast_ref[...] = down_next_ref[...].astype(_bf16)

    @pl.when(t % 2 == 0)
    def _():
        stages(zs0_ref, zs1_ref, act1_ref, act0_ref)

    @pl.when(t % 2 == 1)
    def _():
        stages(zs1_ref, zs0_ref, act0_ref, act1_ref)

    @pl.when((fc < nres) & (t >= 2))
    def _():
        rows = pl.ds(pl.multiple_of(fc * res_rows, res_rows), res_rows)
        o_ref[rows, :] += alpha * hf_ref[...]

    @pl.when((fc == nf - 1) & (t >= 2))
    def _():
        _layer_norm_rows(o_ref, ob_ref, gb_ref, stat_ref)


def _ffn(hb, hf, w_up, w_down, conv_w, conv_b, g, b, alpha, seq, bm, bf, nres,
         up_stack=None, down_stack=None, next_layer=None):
    m, d = hb.shape
    d_ff = w_down.shape[0]
    nf = d_ff // bf
    nblk = m // bm
    nchunks = nblk * nf
    res_rows = bm // nres
    cast_next = next_layer is not None
    assert nres <= nf and seq % bm == 0 and bm % CONV_TILE_ROWS == 0 and res_rows % SUBLANES == 0
    up_rows = down_cols = d // nblk
    assert up_rows % LN_ROWS == 0 and down_cols % 128 == 0
    taps = jnp.concatenate([conv_w, conv_b[None, :]], axis=0).reshape(CONV_W + 1, 2, nf, bf)
    conv_params = jnp.pad(taps.transpose(2, 0, 1, 3).reshape(nf, CONV_W + 1, 2 * bf),
                          ((0, 0), (0, SUBLANES - CONV_W - 1), (0, 0)))
    gain_bias = jnp.stack([g, b])

    steps = np.arange(nchunks + 2)
    chunk_a = np.minimum(steps, nchunks - 1)
    chunk_b = np.clip(steps - 1, 0, nchunks - 1)
    chunk_c = np.maximum(steps - 2, 0)
    tables = [chunk_a // nf, chunk_a % nf, chunk_b % nf, chunk_c // nf, chunk_c % nf,
              ((chunk_a // nf) % (seq // bm) == 0).astype(np.int32)]
    tables = [jnp.asarray(tbl, jnp.int32) for tbl in tables]

    def res_map(t, ia, fa, fb, ic, fc, start):
        return (ic[t] * nres + jnp.minimum(fc[t], nres - 1), 0)

    in_specs = [pl.BlockSpec((bm, d), lambda t, ia, fa, fb, ic, fc, start: (ia[t], 0)),
                pl.BlockSpec((res_rows, d), res_map),
                pl.BlockSpec((d, bf), lambda t, ia, fa, fb, ic, fc, start: (0, fa[t])),
                pl.BlockSpec((d, bf), lambda t, ia, fa, fb, ic, fc, start: (0, nf + fa[t])),
                pl.BlockSpec((None, SUBLANES, 2 * bf),
                             lambda t, ia, fa, fb, ic, fc, start: (fb[t], 0, 0)),
                pl.BlockSpec((bf, d), lambda t, ia, fa, fb, ic, fc, start: (fc[t], 0)),
                pl.BlockSpec((2, d), lambda t, ia, fa, fb, ic, fc, start: (0, 0))]
    out_specs = [pl.BlockSpec((bm, d), lambda t, ia, fa, fb, ic, fc, start: (ic[t], 0)),
                 pl.BlockSpec((bm, d), lambda t, ia, fa, fb, ic, fc, start: (ic[t], 0))]
    out_shape = [jax.ShapeDtypeStruct((m, d), _f32), jax.ShapeDtypeStruct((m, d), _bf16)]
    operands = [hb, hf, w_up, w_up, conv_params, w_down, gain_bias]
    if cast_next:
        in_specs += [pl.BlockSpec((None, up_rows, 2 * bf),
                                  lambda t, ia, fa, fb, ic, fc, start: (next_layer, ia[t], fa[t])),
                     pl.BlockSpec((None, bf, down_cols),
                                  lambda t, ia, fa, fb, ic, fc, start: (next_layer, fa[t], ia[t]))]
        out_specs += [pl.BlockSpec((up_rows, 2 * bf),
                                   lambda t, ia, fa, fb, ic, fc, start: (ia[t], fa[t])),
                      pl.BlockSpec((bf, down_cols),
                                   lambda t, ia, fa, fb, ic, fc, start: (fa[t], ia[t]))]
        out_shape += [jax.ShapeDtypeStruct((d, 2 * d_ff), _bf16),
                      jax.ShapeDtypeStruct((d_ff, d), _bf16)]
        operands += [up_stack, down_stack]

    grid_spec = pltpu.PrefetchScalarGridSpec(
        num_scalar_prefetch=len(tables),
        grid=(nchunks + 2,),
        in_specs=in_specs,
        out_specs=out_specs,
        scratch_shapes=[pltpu.VMEM((2, SUBLANES + bm, bf), _f32),
                        pltpu.VMEM((2, SUBLANES + bm, bf), _f32),
                        pltpu.VMEM((bm, bf), _bf16),
                        pltpu.VMEM((bm, bf), _bf16),
                        pltpu.VMEM((nf, 2, SUBLANES, bf), _f32),
                        pltpu.VMEM((2, bm, 1), _f32)])
    return pl.pallas_call(
        functools.partial(_ffn_kernel, alpha=alpha, nf=nf, nres=nres, cast_next=cast_next),
        grid_spec=grid_spec,
        out_shape=out_shape,
        compiler_params=_params(("arbitrary",)),
    )(*tables, *operands)


def _gelu(z):
    return 0.5 * z * (1.0 + lax.erf(z * (2.0 ** -0.5)))


def _identity(z):
    return z


def kernel(x, mix_in_a, norm_v_a_g, norm_v_a_b, sgu_w, sgu_b, mix_out_a, w_kv, mix_in_b, sinks,
           mix_out_b, ffn_up, ffn_conv_w, ffn_conv_b, ffn_down, ln_g, ln_b):
    batch, seq, d = x.shape
    depth = ffn_up.shape[0]
    n_a = mix_in_a.shape[0]
    alpha = (2.0 * depth) ** 0.25
    scale = HEAD_DIM ** -0.5
    m = batch * seq

    w_in_a = _cast_bf16(mix_in_a, block_rows=256)
    w_in_b = _cast_bf16(mix_in_b, block_rows=512)
    w_kv_b = _cast_bf16(w_kv[None], block_rows=1024)
    w_up = _cast_bf16_layer(ffn_up, 0, block_rows=64)
    w_down = _cast_bf16_layer(ffn_down, 0, block_rows=256)

    out_slab = min(1024, d)

    hf = x.reshape(m, d)
    hb = hf.astype(_bf16)
    kv = None
    for l in range(depth):
        if l < n_a:
            z = _mm_act(hb, w_in_a, l, _gelu, _bf16, bm=1024, bn=1024)
            mix_in = _sgu(z, norm_v_a_g[l], norm_v_a_b[l], sgu_w[l], sgu_b[l], bm=256)
            w_out = _cast_bf16_column_slabs(mix_out_a, l, bn=out_slab, block_rows=512)
        else:
            j = l - n_a
            if kv is None:
                kv = _mm_act(hb, w_kv_b, 0, _identity, _bf16, bm=1024, bn=1024)
            q = _mm_act(hb, w_in_b, j, lambda t: t * scale, _bf16, bm=1024, bn=1024)
            mix_in = _attention(q, kv, sinks[j], batch, seq)
            w_out = _cast_bf16_column_slabs(mix_out_b, j, bn=out_slab, block_rows=512)
        hf, hb = _mm_res_ln(mix_in, w_out, hf, ln_g[l, 0], ln_b[l, 0], alpha, bm=256)
        outs = _ffn(hb, hf, w_up, w_down, ffn_conv_w[l], ffn_conv_b[l], ln_g[l, 1], ln_b[l, 1],
                    alpha, seq, bm=512, bf=256, nres=min(8, ffn_down.shape[1] // 256),
                    up_stack=ffn_up, down_stack=ffn_down,
                    next_layer=l + 1 if l + 1 < depth else None)
        hf, hb = outs[:2]
        if l + 1 < depth:
            w_up, w_down = outs[2:]
    return hf.reshape(batch, seq, d)
```

```python
import functools

import jax
import jax.numpy as jnp
import numpy as np
from jax import lax
from jax.experimental import pallas as pl
from jax.experimental.pallas import tpu as pltpu

LN_EPS = 1e-5
CHUNK = 128
HEAD_DIM = 64
N_KV = 8
WINDOW = 128
CONV_W = 3

VMEM_LIMIT_BYTES = 56 * 1024 * 1024
SUBLANES = 8
LN_ROWS = 16
LN_UNROLL = 8
LN_COLS = 1024
CONV_TILE_ROWS = 32

_f32 = jnp.float32
_bf16 = jnp.bfloat16


def _params(sem):
    return pltpu.CompilerParams(dimension_semantics=sem, vmem_limit_bytes=VMEM_LIMIT_BYTES)


def _layer_norm(y, g, b):
    mu = jnp.mean(y, axis=-1, keepdims=True)
    yc = y - mu
    var = jnp.mean(yc * yc, axis=-1, keepdims=True)
    return yc * lax.rsqrt(var + LN_EPS) * g + b


def _layer_norm_rows(o_ref, ob_ref, gb_ref, stat_ref):
    ngroups = o_ref.shape[0] // LN_ROWS

    def group(r):
        return pl.ds(pl.multiple_of(r * LN_ROWS, LN_ROWS), LN_ROWS)

    def mean_pass(r, carry):
        rows = group(r)
        stat_ref[0, rows, :] = jnp.mean(o_ref[rows, :], axis=-1, keepdims=True)
        return carry

    n = o_ref.shape[1]
    slabs = [slice(c, c + LN_COLS) for c in range(0, n, LN_COLS)]

    def var_pass(r, carry):
        rows = group(r)
        mu = stat_ref[0, rows, :]
        sq = None
        for cols in slabs:
            yc = o_ref[rows, cols] - mu
            sq = yc * yc if sq is None else sq + yc * yc
        ssq = jnp.sum(sq, axis=-1, keepdims=True)
        stat_ref[1, rows, :] = lax.rsqrt(ssq * (1.0 / n) + LN_EPS)
        return carry

    def norm_pass(r, carry):
        rows = group(r)
        mu = stat_ref[0, rows, :]
        inv = stat_ref[1, rows, :]
        for cols in slabs:
            h = (o_ref[rows, cols] - mu) * inv * gb_ref[0:1, cols] + gb_ref[1:2, cols]
            o_ref[rows, cols] = h
            ob_ref[rows, cols] = h.astype(_bf16)
        return carry

    for one_pass in (mean_pass, var_pass, norm_pass):
        lax.fori_loop(0, ngroups, one_pass, 0, unroll=LN_UNROLL)


def _mm_act_kernel(x_ref, w_ref, *refs, act, nside):
    o_ref = refs[nside]
    z = jnp.dot(x_ref[...], w_ref[...], preferred_element_type=_f32)
    o_ref[...] = act(z).astype(o_ref.dtype)
    for src_ref, dst_ref in zip(refs[:nside], refs[nside + 1:]):
        dst_ref[...] = src_ref[...].astype(dst_ref.dtype)


def _mm_act(x, w, layer, act, out_dtype, bm, bn, side_casts=()):
    m, k = x.shape
    n = w.shape[2]
    outs = pl.pallas_call(
        functools.partial(_mm_act_kernel, act=act, nside=len(side_casts)),
        grid=(m // bm, n // bn),
        in_specs=[pl.BlockSpec((bm, k), lambda i, j: (i, 0)),
                  pl.BlockSpec((None, k, bn), lambda i, j: (layer, 0, j))]
                 + [sc[1] for sc in side_casts],
        out_specs=[pl.BlockSpec((bm, bn), lambda i, j: (i, j))] + [sc[3] for sc in side_casts],
        out_shape=[jax.ShapeDtypeStruct((m, n), out_dtype)]
                  + [jax.ShapeDtypeStruct(sc[2], _bf16) for sc in side_casts],
        compiler_params=_params(("arbitrary", "arbitrary")),
    )(x, w, *[sc[0] for sc in side_casts])
    return outs[0] if not side_casts else outs


def _row_tile_cast(src, layer, tile_rows, nj):
    _, r, c = src.shape
    last = r // tile_rows - 1
    assert r % tile_rows == 0

    def tile(i, j):
        return jnp.minimum(i * nj + j, last)

    return (src,
            pl.BlockSpec((None, tile_rows, c), lambda i, j: (layer, tile(i, j), 0)),
            (r, c),
            pl.BlockSpec((tile_rows, c), lambda i, j: (tile(i, j), 0)))


def _column_slab_cast(src, layer, rows, cols, slab):
    _, k, n = src.shape
    per_slab = slab // cols
    return (src,
            pl.BlockSpec((None, rows, cols), lambda i, j: (layer, i, j)),
            (n // slab, k, slab),
            pl.BlockSpec((None, rows, cols), lambda i, j: (j // per_slab, i, j % per_slab)))


def _cast_kernel(x_ref, o_ref):
    o_ref[...] = x_ref[...].astype(o_ref.dtype)


def _cast_bf16(w, block_rows):
    c = w.shape[-1]
    w2 = w.reshape(-1, c)
    r = w2.shape[0]
    assert r % block_rows == 0
    out = pl.pallas_call(
        _cast_kernel,
        grid=(r // block_rows,),
        in_specs=[pl.BlockSpec((block_rows, c), lambda i: (i, 0))],
        out_specs=pl.BlockSpec((block_rows, c), lambda i: (i, 0)),
        out_shape=jax.ShapeDtypeStruct((r, c), _bf16),
        compiler_params=_params(("arbitrary",)),
    )(w2)
    return out.reshape(w.shape)


def _mm_res_ln_kernel(x_ref, w_ref, res_ref, g_ref, b_ref, o_ref, ob_ref, y0_ref, y1_ref,
                      *, alpha, nslab):
    i = pl.program_id(0)
    j = pl.program_id(1)
    bn = w_ref.shape[2]
    rows_out = o_ref.shape[0]

    @pl.when((i == 0) & (j == 0))
    def _():
        y1_ref[...] = jnp.zeros_like(y1_ref)

    def step(y_new, y_done):
        y_new[j] = alpha * res_ref[...] + jnp.dot(x_ref[...], w_ref[j], preferred_element_type=_f32)
        for r in range(0, rows_out, LN_ROWS):
            rows = pl.ds(pl.multiple_of(j * rows_out + r, LN_ROWS), LN_ROWS)
            parts = [y_done[s, rows, :] for s in range(nslab)]
            mu = sum(jnp.sum(p, axis=-1, keepdims=True) for p in parts) * (1.0 / (nslab * bn))
            parts = [p - mu for p in parts]
            var = sum(jnp.sum(p * p, axis=-1, keepdims=True) for p in parts) * (1.0 / (nslab * bn))
            inv = lax.rsqrt(var + LN_EPS)
            for s in range(nslab):
                cols = slice(s * bn, (s + 1) * bn)
                h = parts[s] * inv * g_ref[:, cols] + b_ref[:, cols]
                o_ref[r:r + LN_ROWS, cols] = h
                ob_ref[r:r + LN_ROWS, cols] = h.astype(_bf16)

    @pl.when(i % 2 == 0)
    def _():
        step(y0_ref, y1_ref)

    @pl.when(i % 2 == 1)
    def _():
        step(y1_ref, y0_ref)


def _mm_res_ln(x, w_slabs, res, g, b, alpha, bm):
    m, k = x.shape
    nslab, _, bn = w_slabs.shape
    n = nslab * bn
    nblk = m // bm
    rows_out = bm // nslab
    assert rows_out % LN_ROWS == 0

    def block_in(i):
        return jnp.minimum(i, nblk - 1)

    def out_map(i, j):
        return (jnp.maximum(i - 1, 0) * nslab + j, 0)

    return pl.pallas_call(
        functools.partial(_mm_res_ln_kernel, alpha=alpha, nslab=nslab),
        grid=(nblk + 1, nslab),
        in_specs=[pl.BlockSpec((bm, k), lambda i, j: (block_in(i), 0)),
                  pl.BlockSpec((nslab, k, bn), lambda i, j: (0, 0, 0), pipeline_mode=pl.Buffered(1)),
                  pl.BlockSpec((bm, bn), lambda i, j: (block_in(i), j)),
                  pl.BlockSpec((1, n), lambda i, j: (0, 0)),
                  pl.BlockSpec((1, n), lambda i, j: (0, 0))],
        out_specs=[pl.BlockSpec((rows_out, n), out_map),
                   pl.BlockSpec((rows_out, n), out_map)],
        out_shape=[jax.ShapeDtypeStruct((m, n), _f32),
                   jax.ShapeDtypeStruct((m, n), _bf16)],
        scratch_shapes=[pltpu.VMEM((nslab, bm, bn), _f32),
                        pltpu.VMEM((nslab, bm, bn), _f32)],
        compiler_params=_params(("arbitrary", "arbitrary")),
    )(x, w_slabs, res, g.reshape(1, n), b.reshape(1, n))


def _sgu_kernel(u_ref, v_ref, vg_ref, vb_ref, w_ref, bs_ref, o_ref, *, nchunk, ngroup):
    vn = _layer_norm(v_ref[...].astype(_f32), vg_ref[...], vb_ref[...]).astype(_bf16)
    row = lax.broadcasted_iota(jnp.int32, (CHUNK, CHUNK), 0)
    col = lax.broadcasted_iota(jnp.int32, (CHUNK, CHUNK), 1)
    causal = col <= row
    for g in range(ngroup):
        lanes = slice(g * CHUNK, (g + 1) * CHUNK)
        wg = jnp.where(causal, w_ref[g], 0.0).astype(_bf16)
        rhs = jnp.concatenate(
            [vn[c * CHUNK:(c + 1) * CHUNK, lanes] for c in range(nchunk)], axis=1)
        vm = jnp.dot(wg, rhs, preferred_element_type=_f32) + bs_ref[g]
        for c in range(nchunk):
            rows = slice(c * CHUNK, (c + 1) * CHUNK)
            o_ref[rows, lanes] = (u_ref[rows, lanes].astype(_f32)
                                  * vm[:, c * CHUNK:(c + 1) * CHUNK]).astype(_bf16)


def _sgu(z, vg, vb, w_s, b_s, bm):
    m, n2 = z.shape
    d = n2 // 2
    ngroup = d // CHUNK
    return pl.pallas_call(
        functools.partial(_sgu_kernel, nchunk=bm // CHUNK, ngroup=ngroup),
        grid=(m // bm,),
        in_specs=[pl.BlockSpec((bm, d), lambda i: (i, 0)),
                  pl.BlockSpec((bm, d), lambda i: (i, 1)),
                  pl.BlockSpec((1, d), lambda i: (0, 0)),
                  pl.BlockSpec((1, d), lambda i: (0, 0)),
                  pl.BlockSpec((ngroup, CHUNK, CHUNK), lambda i: (0, 0, 0)),
                  pl.BlockSpec((ngroup, CHUNK, 1), lambda i: (0, 0, 0))],
        out_specs=pl.BlockSpec((bm, d), lambda i: (i, 0)),
        out_shape=jax.ShapeDtypeStruct((m, d), _bf16),
        compiler_params=_params(("arbitrary",)),
    )(z, z, vg.reshape(1, d), vb.reshape(1, d), w_s, b_s.reshape(ngroup, CHUNK, 1))


def _attn_kernel(sink_ref, q_ref, kp_ref, kc_ref, vp_ref, vc_ref, o_ref, *, gqa):
    n = pl.program_id(1)
    blk = q_ref.shape[0]
    npair = gqa // 2
    pair_w = 2 * HEAD_DIM
    lower_kv = lax.broadcasted_iota(jnp.int32, (2 * blk, pair_w), 1) < HEAD_DIM
    key0 = lax.broadcasted_iota(jnp.int32, (2 * blk, pair_w), 0) == 0
    lower_q = lax.broadcasted_iota(jnp.int32, (blk, pair_w), 1) < HEAD_DIM
    qi = lax.broadcasted_iota(jnp.int32, (blk, 2 * blk), 0)
    kj = lax.broadcasted_iota(jnp.int32, (blk, 2 * blk), 1)
    valid = (kj > qi) & (kj <= qi + WINDOW) & ((n > 0) | (kj >= blk))
    bias = jnp.where(valid, 0.0, -jnp.inf)
    sink_col = kj == 0

    def halves(prev_ref, cur_ref, hp, fill):
        lanes = slice(hp * pair_w, (hp + 1) * pair_w)
        both = jnp.concatenate([prev_ref[:, lanes], cur_ref[:, lanes]], axis=0).astype(_f32)
        both = jnp.where(key0, 0.0, both)
        swapped = pltpu.roll(both, HEAD_DIM, 1)
        pad = jnp.full_like(both, fill)
        head_a = (jnp.where(lower_kv, both, pad), jnp.where(lower_kv, pad, swapped))
        head_b = (jnp.where(lower_kv, swapped, pad), jnp.where(lower_kv, pad, both))
        return [tuple(t.astype(_bf16) for t in head_a), tuple(t.astype(_bf16) for t in head_b)]

    for hp in range(N_KV // 2):
        k_heads = halves(kp_ref, kc_ref, hp, 0.0)
        v_heads = halves(vp_ref, vc_ref, hp, 1.0)
        for sub in range(2):
            h = 2 * hp + sub
            base = h * gqa * HEAD_DIM
            qs = jnp.concatenate(
                [q_ref[:, base + p * pair_w: base + (p + 1) * pair_w] for p in range(npair)], axis=0)
            pv = []
            for slot in range(2):
                s_all = lax.dot_general(qs, k_heads[sub][slot], (((1,), (1,)), ((), ())),
                                        preferred_element_type=_f32)
                e_slabs = []
                for p in range(npair):
                    sink = sink_ref[h * gqa + 2 * p + slot]
                    s = s_all[p * blk:(p + 1) * blk] + jnp.where(sink_col, sink, bias)
                    mx = jnp.max(s, axis=-1, keepdims=True)
                    e_slabs.append(jnp.exp(s - mx).astype(_bf16))
                pv.append(jnp.dot(jnp.concatenate(e_slabs, axis=0), v_heads[sub][slot],
                                  preferred_element_type=_f32))
            for p in range(npair):
                a = pv[0][p * blk:(p + 1) * blk]
                b = pv[1][p * blk:(p + 1) * blk]
                num = jnp.where(lower_q, a, b)
                den = pltpu.roll(jnp.where(lower_q, b, a), HEAD_DIM, 1)
                o_ref[:, base + p * pair_w: base + (p + 1) * pair_w] = (num / den).astype(o_ref.dtype)


def _attention(q, kv, sinks, batch, seq):
    m, d = q.shape
    blk = WINDOW
    nb = seq // blk
    kvw = N_KV * HEAD_DIM
    gqa = d // (N_KV * HEAD_DIM)

    def cur(col):
        return lambda b, n: (b * nb + n, col)

    def prev(col):
        return lambda b, n: (b * nb + jnp.maximum(n - 1, 0), col)

    return pl.pallas_call(
        functools.partial(_attn_kernel, gqa=gqa),
        grid=(batch, nb),
        in_specs=[pl.BlockSpec(memory_space=pltpu.SMEM),
                  pl.BlockSpec((blk, d), cur(0)),
                  pl.BlockSpec((blk, kvw), prev(0)),
                  pl.BlockSpec((blk, kvw), cur(0)),
                  pl.BlockSpec((blk, kvw), prev(1)),
                  pl.BlockSpec((blk, kvw), cur(1))],
        out_specs=pl.BlockSpec((blk, d), cur(0)),
        out_shape=jax.ShapeDtypeStruct((m, d), _bf16),
        compiler_params=_params(("arbitrary", "arbitrary")),
    )(sinks, q, kv, kv, kv, kv)


def _ffn_kernel(ia_ref, fa_ref, fb_ref, ic_ref, fc_ref, start_ref,
                hb_ref, hf_ref, wg_ref, wu_ref, cp_ref, wd_ref, gb_ref, *refs,
                alpha, nf, nres, cast_next):
    if cast_next:
        up_next_ref, down_next_ref = refs[:2]
        o_ref, ob_ref, up_cast_ref, down_cast_ref = refs[2:6]
        zs0_ref, zs1_ref, act0_ref, act1_ref, tail_ref, stat_ref = refs[6:]
    else:
        o_ref, ob_ref = refs[:2]
        zs0_ref, zs1_ref, act0_ref, act1_ref, tail_ref, stat_ref = refs[2:]
    t = pl.program_id(0)
    bm = hb_ref.shape[0]
    res_rows = hf_ref.shape[0]
    fa = fa_ref[t]
    seq_start = start_ref[t] == 1
    fc = fc_ref[t]

    @pl.when(t == 0)
    def _():
        zs0_ref[...] = jnp.zeros_like(zs0_ref)
        zs1_ref[...] = jnp.zeros_like(zs1_ref)
        act0_ref[...] = jnp.zeros_like(act0_ref)
        act1_ref[...] = jnp.zeros_like(act1_ref)

    @pl.when(fc == 0)
    def _():
        o_ref[...] = jnp.zeros_like(o_ref)

    bf = wg_ref.shape[1]

    def conv_tile(zs_ref, half, r):
        rows = CONV_TILE_ROWS
        lanes = slice(half * bf, (half + 1) * bf)
        z0 = zs_ref[half, SUBLANES + r:SUBLANES + r + rows, :]
        z1 = zs_ref[half, SUBLANES - 1 + r:SUBLANES - 1 + r + rows, :]
        z2 = zs_ref[half, SUBLANES - 2 + r:SUBLANES - 2 + r + rows, :]
        return (cp_ref[0:1, lanes] * z2 + cp_ref[1:2, lanes] * z1 + cp_ref[2:3, lanes] * z0
                + cp_ref[3:4, lanes])

    def stages(zs_a, zs_b, act_b, act_c):
        hb = hb_ref[...]
        for half, w_ref in ((0, wg_ref), (1, wu_ref)):
            zs_a[half, :SUBLANES, :] = jnp.where(seq_start, 0.0, tail_ref[fa, half])
            zs_a[half, SUBLANES:, :] = jnp.dot(hb, w_ref[...], preferred_element_type=_f32)
            tail_ref[fa, half] = zs_a[half, bm:, :]
        ntile = bm // CONV_TILE_ROWS
        bn = o_ref.shape[1] // ntile
        for k in range(ntile):
            cols = slice(k * bn, (k + 1) * bn)
            o_ref[:, cols] += jnp.dot(act_c[...], wd_ref[:, cols], preferred_element_type=_f32)
            r = k * CONV_TILE_ROWS
            gate = conv_tile(zs_b, 0, r)
            up = conv_tile(zs_b, 1, r)
            act_b[r:r + CONV_TILE_ROWS, :] = (gate * jax.nn.sigmoid(gate) * up).astype(_bf16)
        if cast_next:
            up_cast_ref[...] = up_next_ref[...].astype(_bf16)
            down_cast_ref[...] = down_next_ref[...].astype(_bf16)

    @pl.when(t % 2 == 0)
    def _():
        stages(zs0_ref, zs1_ref, act1_ref, act0_ref)

    @pl.when(t % 2 == 1)
    def _():
        stages(zs1_ref, zs0_ref, act0_ref, act1_ref)

    @pl.when((fc < nres) & (t >= 2))
    def _():
        rows = pl.ds(pl.multiple_of(fc * res_rows, res_rows), res_rows)
        o_ref[rows, :] += alpha * hf_ref[...]

    @pl.when((fc == nf - 1) & (t >= 2))
    def _():
        _layer_norm_rows(o_ref, ob_ref, gb_ref, stat_ref)


def _ffn(hb, hf, w_up, w_down, conv_w, conv_b, g, b, alpha, seq, bm, bf, nres,
         up_stack=None, down_stack=None, next_layer=None):
    m, d = hb.shape
    d_ff = w_down.shape[0]
    nf = d_ff // bf
    nblk = m // bm
    nchunks = nblk * nf
    res_rows = bm // nres
    cast_next = next_layer is not None
    assert nres <= nf and seq % bm == 0 and bm % CONV_TILE_ROWS == 0 and res_rows % SUBLANES == 0
    up_rows = down_cols = d // nblk
    assert up_rows % LN_ROWS == 0 and down_cols % 128 == 0
    taps = jnp.concatenate([conv_w, conv_b[None, :]], axis=0).reshape(CONV_W + 1, 2, nf, bf)
    conv_params = jnp.pad(taps.transpose(2, 0, 1, 3).reshape(nf, CONV_W + 1, 2 * bf),
                          ((0, 0), (0, SUBLANES - CONV_W - 1), (0, 0)))
    gain_bias = jnp.stack([g, b])

    steps = np.arange(nchunks + 2)
    chunk_a = np.minimum(steps, nchunks - 1)
    chunk_b = np.clip(steps - 1, 0, nchunks - 1)
    chunk_c = np.maximum(steps - 2, 0)
    tables = [chunk_a // nf, chunk_a % nf, chunk_b % nf, chunk_c // nf, chunk_c % nf,
              ((chunk_a // nf) % (seq // bm) == 0).astype(np.int32)]
    tables = [jnp.asarray(tbl, jnp.int32) for tbl in tables]

    def res_map(t, ia, fa, fb, ic, fc, start):
        return (ic[t] * nres + jnp.minimum(fc[t], nres - 1), 0)

    in_specs = [pl.BlockSpec((bm, d), lambda t, ia, fa, fb, ic, fc, start: (ia[t], 0)),
                pl.BlockSpec((res_rows, d), res_map),
                pl.BlockSpec((d, bf), lambda t, ia, fa, fb, ic, fc, start: (0, fa[t])),
                pl.BlockSpec((d, bf), lambda t, ia, fa, fb, ic, fc, start: (0, nf + fa[t])),
                pl.BlockSpec((None, SUBLANES, 2 * bf),
                             lambda t, ia, fa, fb, ic, fc, start: (fb[t], 0, 0)),
                pl.BlockSpec((bf, d), lambda t, ia, fa, fb, ic, fc, start: (fc[t], 0)),
                pl.BlockSpec((2, d), lambda t, ia, fa, fb, ic, fc, start: (0, 0))]
    out_specs = [pl.BlockSpec((bm, d), lambda t, ia, fa, fb, ic, fc, start: (ic[t], 0)),
                 pl.BlockSpec((bm, d), lambda t, ia, fa, fb, ic, fc, start: (ic[t], 0))]
    out_shape = [jax.ShapeDtypeStruct((m, d), _f32), jax.ShapeDtypeStruct((m, d), _bf16)]
    operands = [hb, hf, w_up, w_up, conv_params, w_down, gain_bias]
    if cast_next:
        in_specs += [pl.BlockSpec((None, up_rows, 2 * bf),
                                  lambda t, ia, fa, fb, ic, fc, start: (next_layer, ia[t], fa[t])),
                     pl.BlockSpec((None, bf, down_cols),
                                  lambda t, ia, fa, fb, ic, fc, start: (next_layer, fa[t], ia[t]))]
        out_specs += [pl.BlockSpec((up_rows, 2 * bf),
                                   lambda t, ia, fa, fb, ic, fc, start: (ia[t], fa[t])),
                      pl.BlockSpec((bf, down_cols),
                                   lambda t, ia, fa, fb, ic, fc, start: (fa[t], ia[t]))]
        out_shape += [jax.ShapeDtypeStruct((d, 2 * d_ff), _bf16),
                      jax.ShapeDtypeStruct((d_ff, d), _bf16)]
        operands += [up_stack, down_stack]

    grid_spec = pltpu.PrefetchScalarGridSpec(
        num_scalar_prefetch=len(tables),
        grid=(nchunks + 2,),
        in_specs=in_specs,
        out_specs=out_specs,
        scratch_shapes=[pltpu.VMEM((2, SUBLANES + bm, bf), _f32),
                        pltpu.VMEM((2, SUBLANES + bm, bf), _f32),
                        pltpu.VMEM((bm, bf), _bf16),
                        pltpu.VMEM((bm, bf), _bf16),
                        pltpu.VMEM((nf, 2, SUBLANES, bf), _f32),
                        pltpu.VMEM((2, bm, 1), _f32)])
    return pl.pallas_call(
        functools.partial(_ffn_kernel, alpha=alpha, nf=nf, nres=nres, cast_next=cast_next),
        grid_spec=grid_spec,
        out_shape=out_shape,
        compiler_params=_params(("arbitrary",)),
    )(*tables, *operands)


def _gelu(z):
    return 0.5 * z * (1.0 + lax.erf(z * (2.0 ** -0.5)))


def _identity(z):
    return z


def kernel(x, mix_in_a, norm_v_a_g, norm_v_a_b, sgu_w, sgu_b, mix_out_a, w_kv, mix_in_b, sinks,
           mix_out_b, ffn_up, ffn_conv_w, ffn_conv_b, ffn_down, ln_g, ln_b):
    batch, seq, d = x.shape
    depth = ffn_up.shape[0]
    n_a = mix_in_a.shape[0]
    alpha = (2.0 * depth) ** 0.25
    scale = HEAD_DIM ** -0.5
    m = batch * seq

    assert n_a >= 1
    bm_in, bn_in = 1024, 1024
    row_blocks = m // bm_in
    out_slab = min(1024, d)
    w_in_a = _cast_bf16(mix_in_a, block_rows=256)

    def row_tiles(src, steps):
        r = src.shape[1]
        return next(t for t in range(LN_ROWS, r + 1, LN_ROWS) if r % t == 0 and r // t <= steps)

    def slab_cast(src, layer, nj):
        return _column_slab_cast(src, layer, src.shape[1] // row_blocks, src.shape[2] // nj, out_slab)

    hf = x.reshape(m, d)
    hb = hf.astype(_bf16)
    kv = None
    for l in range(depth):
        if l < n_a:
            bn_a = bn_in // 2 if l == 0 else bn_in
            nj = mix_in_a.shape[2] // bn_a
            steps = row_blocks * nj
            sides = [slab_cast(mix_out_a, l, nj)]
            if l == 0:
                sides += [_row_tile_cast(ffn_up, 0, row_tiles(ffn_up, steps), nj),
                          _row_tile_cast(ffn_down, 0, row_tiles(ffn_down, steps), nj)]
            if l == n_a - 1:
                q_stack = mix_in_b.reshape(1, -1, mix_in_b.shape[2])
                kv_stack = w_kv[None]
                sides += [_row_tile_cast(q_stack, 0, row_tiles(q_stack, steps), nj),
                          _row_tile_cast(kv_stack, 0, row_tiles(kv_stack, steps), nj)]
            outs = _mm_act(hb, w_in_a, l, _gelu, _bf16, bm=bm_in, bn=bn_a, side_casts=sides)
            z, w_out, rest = outs[0], outs[1], list(outs[2:])
            if l == 0:
                w_up, w_down = rest[:2]
                rest = rest[2:]
            if l == n_a - 1:
                w_in_b = rest[0].reshape(mix_in_b.shape)
                w_kv_b = rest[1][None]
            mix_in = _sgu(z, norm_v_a_g[l], norm_v_a_b[l], sgu_w[l], sgu_b[l], bm=256)
        else:
            j = l - n_a
            if kv is None:
                kv = _mm_act(hb, w_kv_b, 0, _identity, _bf16, bm=bm_in, bn=bn_in)
            nj = mix_in_b.shape[2] // bn_in
            q, w_out = _mm_act(hb, w_in_b, j, lambda t: t * scale, _bf16, bm=bm_in, bn=bn_in,
                               side_casts=[slab_cast(mix_out_b, j, nj)])
            mix_in = _attention(q, kv, sinks[j], batch, seq)
        hf, hb = _mm_res_ln(mix_in, w_out, hf, ln_g[l, 0], ln_b[l, 0], alpha, bm=256)
        outs = _ffn(hb, hf, w_up, w_down, ffn_conv_w[l], ffn_conv_b[l], ln_g[l, 1], ln_b[l, 1],
                    alpha, seq, bm=512, bf=256, nres=min(8, ffn_down.shape[1] // 256),
                    up_stack=ffn_up, down_stack=ffn_down,
                    next_layer=l + 1 if l + 1 < depth else None)
        hf, hb = outs[:2]
        if l + 1 < depth:
            w_up, w_down = outs[2:]
    return hf.reshape(batch, seq, d)
```

```python
import functools

import jax
import jax.numpy as jnp
import numpy as np
from jax import lax
from jax.experimental import pallas as pl
from jax.experimental.pallas import tpu as pltpu

LN_EPS = 1e-5
CHUNK = 128
HEAD_DIM = 64
N_KV = 8
WINDOW = 128
CONV_W = 3

VMEM_LIMIT_BYTES = 56 * 1024 * 1024
SUBLANES = 8
LN_ROWS = 16
LN_UNROLL = 8
LN_COLS = 1024
CONV_TILE_ROWS = 32

_f32 = jnp.float32
_bf16 = jnp.bfloat16


def _params(sem):
    return pltpu.CompilerParams(dimension_semantics=sem, vmem_limit_bytes=VMEM_LIMIT_BYTES)


def _layer_norm(y, g, b):
    mu = jnp.mean(y, axis=-1, keepdims=True)
    yc = y - mu
    var = jnp.mean(yc * yc, axis=-1, keepdims=True)
    return yc * lax.rsqrt(var + LN_EPS) * g + b


def _layer_norm_rows(o_ref, ob_ref, gb_ref, stat_ref):
    ngroups = o_ref.shape[0] // LN_ROWS

    def group(r):
        return pl.ds(pl.multiple_of(r * LN_ROWS, LN_ROWS), LN_ROWS)

    def mean_pass(r, carry):
        rows = group(r)
        stat_ref[0, rows, :] = jnp.mean(o_ref[rows, :], axis=-1, keepdims=True)
        return carry

    n = o_ref.shape[1]
    slabs = [slice(c, c + LN_COLS) for c in range(0, n, LN_COLS)]

    def var_pass(r, carry):
        rows = group(r)
        mu = stat_ref[0, rows, :]
        sq = None
        for cols in slabs:
            yc = o_ref[rows, cols] - mu
            sq = yc * yc if sq is None else sq + yc * yc
        ssq = jnp.sum(sq, axis=-1, keepdims=True)
        stat_ref[1, rows, :] = lax.rsqrt(ssq * (1.0 / n) + LN_EPS)
        return carry

    def norm_pass(r, carry):
        rows = group(r)
        mu = stat_ref[0, rows, :]
        inv = stat_ref[1, rows, :]
        for cols in slabs:
            h = (o_ref[rows, cols] - mu) * inv * gb_ref[0:1, cols] + gb_ref[1:2, cols]
            o_ref[rows, cols] = h
            ob_ref[rows, cols] = h.astype(_bf16)
        return carry

    for one_pass in (mean_pass, var_pass, norm_pass):
        lax.fori_loop(0, ngroups, one_pass, 0, unroll=LN_UNROLL)


def _mm_act_kernel(x_ref, w_ref, *refs, act, nside):
    o_ref = refs[nside]
    z = jnp.dot(x_ref[...], w_ref[...], preferred_element_type=_f32)
    o_ref[...] = act(z).astype(o_ref.dtype)
    for src_ref, dst_ref in zip(refs[:nside], refs[nside + 1:]):
        dst_ref[...] = src_ref[...].astype(dst_ref.dtype)


def _mm_act(x, w, layer, act, out_dtype, bm, bn, side_casts=()):
    m, k = x.shape
    n = w.shape[2]
    outs = pl.pallas_call(
        functools.partial(_mm_act_kernel, act=act, nside=len(side_casts)),
        grid=(m // bm, n // bn),
        in_specs=[pl.BlockSpec((bm, k), lambda i, j: (i, 0)),
                  pl.BlockSpec((None, k, bn), lambda i, j: (layer, 0, j))]
                 + [sc[1] for sc in side_casts],
        out_specs=[pl.BlockSpec((bm, bn), lambda i, j: (i, j))] + [sc[3] for sc in side_casts],
        out_shape=[jax.ShapeDtypeStruct((m, n), out_dtype)]
                  + [jax.ShapeDtypeStruct(sc[2], _bf16) for sc in side_casts],
        compiler_params=_params(("arbitrary", "arbitrary")),
    )(x, w, *[sc[0] for sc in side_casts])
    return outs[0] if not side_casts else outs


def _row_tile_cast(src, layer, tile_rows, nj):
    _, r, c = src.shape
    last = r // tile_rows - 1
    assert r % tile_rows == 0

    def tile(i, j):
        return jnp.minimum(i * nj + j, last)

    return (src,
            pl.BlockSpec((None, tile_rows, c), lambda i, j: (layer, tile(i, j), 0)),
            (r, c),
            pl.BlockSpec((tile_rows, c), lambda i, j: (tile(i, j), 0)))


def _column_slab_cast(src, layer, rows, cols, slab):
    _, k, n = src.shape
    per_slab = slab // cols
    return (src,
            pl.BlockSpec((None, rows, cols), lambda i, j: (layer, i, j)),
            (n // slab, k, slab),
            pl.BlockSpec((None, rows, cols), lambda i, j: (j // per_slab, i, j % per_slab)))


def _cast_kernel(x_ref, o_ref):
    o_ref[...] = x_ref[...].astype(o_ref.dtype)


def _cast_bf16(w, block_rows):
    c = w.shape[-1]
    w2 = w.reshape(-1, c)
    r = w2.shape[0]
    assert r % block_rows == 0
    out = pl.pallas_call(
        _cast_kernel,
        grid=(r // block_rows,),
        in_specs=[pl.BlockSpec((block_rows, c), lambda i: (i, 0))],
        out_specs=pl.BlockSpec((block_rows, c), lambda i: (i, 0)),
        out_shape=jax.ShapeDtypeStruct((r, c), _bf16),
        compiler_params=_params(("arbitrary",)),
    )(w2)
    return out.reshape(w.shape)


def _mm_res_ln_kernel(x_ref, w_ref, res_ref, g_ref, b_ref, o_ref, ob_ref, y0_ref, y1_ref,
                      *, alpha, nslab):
    i = pl.program_id(0)
    j = pl.program_id(1)
    bn = w_ref.shape[2]
    rows_out = o_ref.shape[0]

    @pl.when((i == 0) & (j == 0))
    def _():
        y1_ref[...] = jnp.zeros_like(y1_ref)

    def step(y_new, y_done):
        y_new[j] = alpha * res_ref[...] + jnp.dot(x_ref[...], w_ref[j], preferred_element_type=_f32)
        for r in range(0, rows_out, LN_ROWS):
            rows = pl.ds(pl.multiple_of(j * rows_out + r, LN_ROWS), LN_ROWS)
            parts = [y_done[s, rows, :] for s in range(nslab)]
            mu = sum(jnp.sum(p, axis=-1, keepdims=True) for p in parts) * (1.0 / (nslab * bn))
            parts = [p - mu for p in parts]
            var = sum(jnp.sum(p * p, axis=-1, keepdims=True) for p in parts) * (1.0 / (nslab * bn))
            inv = lax.rsqrt(var + LN_EPS)
            for s in range(nslab):
                cols = slice(s * bn, (s + 1) * bn)
                h = parts[s] * inv * g_ref[:, cols] + b_ref[:, cols]
                o_ref[r:r + LN_ROWS, cols] = h
                ob_ref[r:r + LN_ROWS, cols] = h.astype(_bf16)

    @pl.when(i % 2 == 0)
    def _():
        step(y0_ref, y1_ref)

    @pl.when(i % 2 == 1)
    def _():
        step(y1_ref, y0_ref)


def _mm_res_ln(x, w_slabs, res, g, b, alpha, bm):
    m, k = x.shape
    nslab, _, bn = w_slabs.shape
    n = nslab * bn
    nblk = m // bm
    rows_out = bm // nslab
    assert rows_out % LN_ROWS == 0

    def block_in(i):
        return jnp.minimum(i, nblk - 1)

    def out_map(i, j):
        return (jnp.maximum(i - 1, 0) * nslab + j, 0)

    return pl.pallas_call(
        functools.partial(_mm_res_ln_kernel, alpha=alpha, nslab=nslab),
        grid=(nblk + 1, nslab),
        in_specs=[pl.BlockSpec((bm, k), lambda i, j: (block_in(i), 0)),
                  pl.BlockSpec((nslab, k, bn), lambda i, j: (0, 0, 0), pipeline_mode=pl.Buffered(1)),
                  pl.BlockSpec((bm, bn), lambda i, j: (block_in(i), j)),
                  pl.BlockSpec(memory_space=pltpu.VMEM),
                  pl.BlockSpec(memory_space=pltpu.VMEM)],
        out_specs=[pl.BlockSpec((rows_out, n), out_map),
                   pl.BlockSpec((rows_out, n), out_map)],
        out_shape=[jax.ShapeDtypeStruct((m, n), _f32),
                   jax.ShapeDtypeStruct((m, n), _bf16)],
        scratch_shapes=[pltpu.VMEM((nslab, bm, bn), _f32),
                        pltpu.VMEM((nslab, bm, bn), _f32)],
        compiler_params=_params(("arbitrary", "arbitrary")),
    )(x, w_slabs, res, g.reshape(1, n), b.reshape(1, n))


def _sgu_kernel(u_ref, v_ref, vg_ref, vb_ref, w_ref, bs_ref, o_ref, *, nchunk, ngroup):
    vn = _layer_norm(v_ref[...].astype(_f32), vg_ref[...], vb_ref[...]).astype(_bf16)
    row = lax.broadcasted_iota(jnp.int32, (CHUNK, CHUNK), 0)
    col = lax.broadcasted_iota(jnp.int32, (CHUNK, CHUNK), 1)
    causal = col <= row
    for g in range(ngroup):
        lanes = slice(g * CHUNK, (g + 1) * CHUNK)
        wg = jnp.where(causal, w_ref[g], 0.0).astype(_bf16)
        rhs = jnp.concatenate(
            [vn[c * CHUNK:(c + 1) * CHUNK, lanes] for c in range(nchunk)], axis=1)
        vm = jnp.dot(wg, rhs, preferred_element_type=_f32) + bs_ref[g]
        for c in range(nchunk):
            rows = slice(c * CHUNK, (c + 1) * CHUNK)
            o_ref[rows, lanes] = (u_ref[rows, lanes].astype(_f32)
                                  * vm[:, c * CHUNK:(c + 1) * CHUNK]).astype(_bf16)


def _sgu(z, vg, vb, w_s, b_s, bm):
    m, n2 = z.shape
    d = n2 // 2
    ngroup = d // CHUNK
    return pl.pallas_call(
        functools.partial(_sgu_kernel, nchunk=bm // CHUNK, ngroup=ngroup),
        grid=(m // bm,),
        in_specs=[pl.BlockSpec((bm, d), lambda i: (i, 0)),
                  pl.BlockSpec((bm, d), lambda i: (i, 1)),
                  pl.BlockSpec((1, d), lambda i: (0, 0)),
                  pl.BlockSpec((1, d), lambda i: (0, 0)),
                  pl.BlockSpec((ngroup, CHUNK, CHUNK), lambda i: (0, 0, 0)),
                  pl.BlockSpec((ngroup, CHUNK, 1), lambda i: (0, 0, 0))],
        out_specs=pl.BlockSpec((bm, d), lambda i: (i, 0)),
        out_shape=jax.ShapeDtypeStruct((m, d), _bf16),
        compiler_params=_params(("arbitrary",)),
    )(z, z, vg.reshape(1, d), vb.reshape(1, d), w_s, b_s.reshape(ngroup, CHUNK, 1))


def _attn_kernel(sink_ref, q_ref, kp_ref, kc_ref, vp_ref, vc_ref, o_ref, *, gqa):
    n = pl.program_id(1)
    blk = q_ref.shape[0]
    npair = gqa // 2
    pair_w = 2 * HEAD_DIM
    lower_kv = lax.broadcasted_iota(jnp.int32, (2 * blk, pair_w), 1) < HEAD_DIM
    key0 = lax.broadcasted_iota(jnp.int32, (2 * blk, pair_w), 0) == 0
    lower_q = lax.broadcasted_iota(jnp.int32, (blk, pair_w), 1) < HEAD_DIM
    qi = lax.broadcasted_iota(jnp.int32, (blk, 2 * blk), 0)
    kj = lax.broadcasted_iota(jnp.int32, (blk, 2 * blk), 1)
    valid = (kj > qi) & (kj <= qi + WINDOW) & ((n > 0) | (kj >= blk))
    bias = jnp.where(valid, 0.0, -jnp.inf)
    sink_col = kj == 0

    def halves(prev_ref, cur_ref, hp, fill):
        lanes = slice(hp * pair_w, (hp + 1) * pair_w)
        both = jnp.concatenate([prev_ref[:, lanes], cur_ref[:, lanes]], axis=0).astype(_f32)
        both = jnp.where(key0, 0.0, both)
        swapped = pltpu.roll(both, HEAD_DIM, 1)
        pad = jnp.full_like(both, fill)
        head_a = (jnp.where(lower_kv, both, pad), jnp.where(lower_kv, pad, swapped))
        head_b = (jnp.where(lower_kv, swapped, pad), jnp.where(lower_kv, pad, both))
        return [tuple(t.astype(_bf16) for t in head_a), tuple(t.astype(_bf16) for t in head_b)]

    for hp in range(N_KV // 2):
        k_heads = halves(kp_ref, kc_ref, hp, 0.0)
        v_heads = halves(vp_ref, vc_ref, hp, 1.0)
        for sub in range(2):
            h = 2 * hp + sub
            base = h * gqa * HEAD_DIM
            qs = jnp.concatenate(
                [q_ref[:, base + p * pair_w: base + (p + 1) * pair_w] for p in range(npair)], axis=0)
            pv = []
            for slot in range(2):
                s_all = lax.dot_general(qs, k_heads[sub][slot], (((1,), (1,)), ((), ())),
                                        preferred_element_type=_f32)
                e_slabs = []
                for p in range(npair):
                    sink = sink_ref[h * gqa + 2 * p + slot]
                    s = s_all[p * blk:(p + 1) * blk] + jnp.where(sink_col, sink, bias)
                    mx = jnp.max(s, axis=-1, keepdims=True)
                    e_slabs.append(jnp.exp(s - mx).astype(_bf16))
                pv.append(jnp.dot(jnp.concatenate(e_slabs, axis=0), v_heads[sub][slot],
                                  preferred_element_type=_f32))
            for p in range(npair):
                a = pv[0][p * blk:(p + 1) * blk]
                b = pv[1][p * blk:(p + 1) * blk]
                num = jnp.where(lower_q, a, b)
                den = pltpu.roll(jnp.where(lower_q, b, a), HEAD_DIM, 1)
                o_ref[:, base + p * pair_w: base + (p + 1) * pair_w] = (num / den).astype(o_ref.dtype)


def _attention(q, kv, sinks, batch, seq):
    m, d = q.shape
    blk = WINDOW
    nb = seq // blk
    kvw = N_KV * HEAD_DIM
    gqa = d // (N_KV * HEAD_DIM)

    def cur(col):
        return lambda b, n: (b * nb + n, col)

    def prev(col):
        return lambda b, n: (b * nb + jnp.maximum(n - 1, 0), col)

    return pl.pallas_call(
        functools.partial(_attn_kernel, gqa=gqa),
        grid=(batch, nb),
        in_specs=[pl.BlockSpec(memory_space=pltpu.SMEM),
                  pl.BlockSpec((blk, d), cur(0)),
                  pl.BlockSpec((blk, kvw), prev(0)),
                  pl.BlockSpec((blk, kvw), cur(0)),
                  pl.BlockSpec((blk, kvw), prev(1)),
                  pl.BlockSpec((blk, kvw), cur(1))],
        out_specs=pl.BlockSpec((blk, d), cur(0)),
        out_shape=jax.ShapeDtypeStruct((m, d), _bf16),
        compiler_params=_params(("arbitrary", "arbitrary")),
    )(sinks, q, kv, kv, kv, kv)


def _ffn_kernel(ia_ref, fa_ref, fb_ref, ic_ref, fc_ref, start_ref,
                hb_ref, hf_ref, wg_ref, wu_ref, cp_ref, wd_ref, gb_ref, *refs,
                alpha, nf, nres, cast_next):
    if cast_next:
        up_next_ref, down_next_ref = refs[:2]
        o_ref, ob_ref, up_cast_ref, down_cast_ref = refs[2:6]
        zs0_ref, zs1_ref, act0_ref, act1_ref, tail_ref, stat_ref = refs[6:]
    else:
        o_ref, ob_ref = refs[:2]
        zs0_ref, zs1_ref, act0_ref, act1_ref, tail_ref, stat_ref = refs[2:]
    t = pl.program_id(0)
    bm = hb_ref.shape[0]
    res_rows = hf_ref.shape[0]
    fa = fa_ref[t]
    fb = fb_ref[t]
    seq_start = start_ref[t] == 1
    fc = fc_ref[t]

    @pl.when(t == 0)
    def _():
        zs0_ref[...] = jnp.zeros_like(zs0_ref)
        zs1_ref[...] = jnp.zeros_like(zs1_ref)
        act0_ref[...] = jnp.zeros_like(act0_ref)
        act1_ref[...] = jnp.zeros_like(act1_ref)

    @pl.when(fc == 0)
    def _():
        o_ref[...] = jnp.zeros_like(o_ref)

    bf = wg_ref.shape[1]

    def conv_tile(zs_ref, half, r):
        rows = CONV_TILE_ROWS
        lanes = slice(half * bf, (half + 1) * bf)
        z0 = zs_ref[half, SUBLANES + r:SUBLANES + r + rows, :]
        z1 = zs_ref[half, SUBLANES - 1 + r:SUBLANES - 1 + r + rows, :]
        z2 = zs_ref[half, SUBLANES - 2 + r:SUBLANES - 2 + r + rows, :]
        return (cp_ref[fb, 0:1, lanes] * z2 + cp_ref[fb, 1:2, lanes] * z1
                + cp_ref[fb, 2:3, lanes] * z0 + cp_ref[fb, 3:4, lanes])

    def stages(zs_a, zs_b, act_b, act_c):
        hb = hb_ref[...]
        for half, w_ref in ((0, wg_ref), (1, wu_ref)):
            zs_a[half, :SUBLANES, :] = jnp.where(seq_start, 0.0, tail_ref[fa, half])
            zs_a[half, SUBLANES:, :] = jnp.dot(hb, w_ref[...], preferred_element_type=_f32)
            tail_ref[fa, half] = zs_a[half, bm:, :]
        ntile = bm // CONV_TILE_ROWS
        bn = o_ref.shape[1] // ntile
        for k in range(ntile):
            cols = slice(k * bn, (k + 1) * bn)
            o_ref[:, cols] += jnp.dot(act_c[...], wd_ref[:, cols], preferred_element_type=_f32)
            r = k * CONV_TILE_ROWS
            gate = conv_tile(zs_b, 0, r)
            up = conv_tile(zs_b, 1, r)
            act_b[r:r + CONV_TILE_ROWS, :] = (gate * jax.nn.sigmoid(gate) * up).astype(_bf16)
        if cast_next:
            up_cast_ref[...] = up_next_ref[...].astype(_bf16)
            down_cast_ref[...] = down_next_ref[...].astype(_bf16)

    @pl.when(t % 2 == 0)
    def _():
        stages(zs0_ref, zs1_ref, act1_ref, act0_ref)

    @pl.when(t % 2 == 1)
    def _():
        stages(zs1_ref, zs0_ref, act0_ref, act1_ref)

    @pl.when((fc < nres) & (t >= 2))
    def _():
        rows = pl.ds(pl.multiple_of(fc * res_rows, res_rows), res_rows)
        o_ref[rows, :] += alpha * hf_ref[...]

    @pl.when((fc == nf - 1) & (t >= 2))
    def _():
        _layer_norm_rows(o_ref, ob_ref, gb_ref, stat_ref)


def _ffn(hb, hf, w_up, w_down, conv_w, conv_b, g, b, alpha, seq, bm, bf, nres,
         up_stack=None, down_stack=None, next_layer=None):
    m, d = hb.shape
    d_ff = w_down.shape[0]
    nf = d_ff // bf
    nblk = m // bm
    nchunks = nblk * nf
    res_rows = bm // nres
    cast_next = next_layer is not None
    assert nres <= nf and seq % bm == 0 and bm % CONV_TILE_ROWS == 0 and res_rows % SUBLANES == 0
    up_rows = down_cols = d // nblk
    assert up_rows % LN_ROWS == 0 and down_cols % 128 == 0
    taps = jnp.concatenate([conv_w, conv_b[None, :]], axis=0).reshape(CONV_W + 1, 2, nf, bf)
    conv_params = jnp.pad(taps.transpose(2, 0, 1, 3).reshape(nf, CONV_W + 1, 2 * bf),
                          ((0, 0), (0, SUBLANES - CONV_W - 1), (0, 0)))
    gain_bias = jnp.stack([g, b])

    steps = np.arange(nchunks + 2)
    chunk_a = np.minimum(steps, nchunks - 1)
    chunk_b = np.clip(steps - 1, 0, nchunks - 1)
    chunk_c = np.maximum(steps - 2, 0)
    tables = [chunk_a // nf, chunk_a % nf, chunk_b % nf, chunk_c // nf, chunk_c % nf,
              ((chunk_a // nf) % (seq // bm) == 0).astype(np.int32)]
    tables = [jnp.asarray(tbl, jnp.int32) for tbl in tables]

    def res_map(t, ia, fa, fb, ic, fc, start):
        return (ic[t] * nres + jnp.minimum(fc[t], nres - 1), 0)

    in_specs = [pl.BlockSpec((bm, d), lambda t, ia, fa, fb, ic, fc, start: (ia[t], 0)),
                pl.BlockSpec((res_rows, d), res_map),
                pl.BlockSpec((d, bf), lambda t, ia, fa, fb, ic, fc, start: (0, fa[t])),
                pl.BlockSpec((d, bf), lambda t, ia, fa, fb, ic, fc, start: (0, nf + fa[t])),
                pl.BlockSpec(memory_space=pltpu.VMEM),
                pl.BlockSpec((bf, d), lambda t, ia, fa, fb, ic, fc, start: (fc[t], 0)),
                pl.BlockSpec(memory_space=pltpu.VMEM)]
    out_specs = [pl.BlockSpec((bm, d), lambda t, ia, fa, fb, ic, fc, start: (ic[t], 0)),
                 pl.BlockSpec((bm, d), lambda t, ia, fa, fb, ic, fc, start: (ic[t], 0))]
    out_shape = [jax.ShapeDtypeStruct((m, d), _f32), jax.ShapeDtypeStruct((m, d), _bf16)]
    operands = [hb, hf, w_up, w_up, conv_params, w_down, gain_bias]
    if cast_next:
        in_specs += [pl.BlockSpec((None, up_rows, 2 * bf),
                                  lambda t, ia, fa, fb, ic, fc, start: (next_layer, ia[t], fa[t])),
                     pl.BlockSpec((None, bf, down_cols),
                                  lambda t, ia, fa, fb, ic, fc, start: (next_layer, fa[t], ia[t]))]
        out_specs += [pl.BlockSpec((up_rows, 2 * bf),
                                   lambda t, ia, fa, fb, ic, fc, start: (ia[t], fa[t])),
                      pl.BlockSpec((bf, down_cols),
                                   lambda t, ia, fa, fb, ic, fc, start: (fa[t], ia[t]))]
        out_shape += [jax.ShapeDtypeStruct((d, 2 * d_ff), _bf16),
                      jax.ShapeDtypeStruct((d_ff, d), _bf16)]
        operands += [up_stack, down_stack]

    grid_spec = pltpu.PrefetchScalarGridSpec(
        num_scalar_prefetch=len(tables),
        grid=(nchunks + 2,),
        in_specs=in_specs,
        out_specs=out_specs,
        scratch_shapes=[pltpu.VMEM((2, SUBLANES + bm, bf), _f32),
                        pltpu.VMEM((2, SUBLANES + bm, bf), _f32),
                        pltpu.VMEM((bm, bf), _bf16),
                        pltpu.VMEM((bm, bf), _bf16),
                        pltpu.VMEM((nf, 2, SUBLANES, bf), _f32),
                        pltpu.VMEM((2, bm, 1), _f32)])
    return pl.pallas_call(
        functools.partial(_ffn_kernel, alpha=alpha, nf=nf, nres=nres, cast_next=cast_next),
        grid_spec=grid_spec,
        out_shape=out_shape,
        compiler_params=_params(("arbitrary",)),
    )(*tables, *operands)


def _gelu(z):
    return 0.5 * z * (1.0 + lax.erf(z * (2.0 ** -0.5)))


def _identity(z):
    return z


def kernel(x, mix_in_a, norm_v_a_g, norm_v_a_b, sgu_w, sgu_b, mix_out_a, w_kv, mix_in_b, sinks,
           mix_out_b, ffn_up, ffn_conv_w, ffn_conv_b, ffn_down, ln_g, ln_b):
    batch, seq, d = x.shape
    depth = ffn_up.shape[0]
    n_a = mix_in_a.shape[0]
    alpha = (2.0 * depth) ** 0.25
    scale = HEAD_DIM ** -0.5
    m = batch * seq

    assert n_a >= 1
    bm_in, bn_in = 1024, 1024
    row_blocks = m // bm_in
    out_slab = min(1024, d)
    w_in_a = _cast_bf16(mix_in_a, block_rows=256)

    def row_tiles(src, steps):
        r = src.shape[1]
        return next(t for t in range(LN_ROWS, r + 1, LN_ROWS) if r % t == 0 and r // t <= steps)

    def slab_cast(src, layer, nj):
        return _column_slab_cast(src, layer, src.shape[1] // row_blocks, src.shape[2] // nj, out_slab)

    hf = x.reshape(m, d)
    hb = hf.astype(_bf16)
    kv = None
    for l in range(depth):
        if l < n_a:
            bn_a = bn_in // 2 if l == 0 else bn_in
            nj = mix_in_a.shape[2] // bn_a
            steps = row_blocks * nj
            sides = [slab_cast(mix_out_a, l, nj)]
            if l == 0:
                sides += [_row_tile_cast(ffn_up, 0, row_tiles(ffn_up, steps), nj),
                          _row_tile_cast(ffn_down, 0, row_tiles(ffn_down, steps), nj)]
            if l == n_a - 1:
                q_stack = mix_in_b.reshape(1, -1, mix_in_b.shape[2])
                kv_stack = w_kv[None]
                sides += [_row_tile_cast(q_stack, 0, row_tiles(q_stack, steps), nj),
                          _row_tile_cast(kv_stack, 0, row_tiles(kv_stack, steps), nj)]
            outs = _mm_act(hb, w_in_a, l, _gelu, _bf16, bm=bm_in, bn=bn_a, side_casts=sides)
            z, w_out, rest = outs[0], outs[1], list(outs[2:])
            if l == 0:
                w_up, w_down = rest[:2]
                rest = rest[2:]
            if l == n_a - 1:
                w_in_b = rest[0].reshape(mix_in_b.shape)
                w_kv_b = rest[1][None]
            mix_in = _sgu(z, norm_v_a_g[l], norm_v_a_b[l], sgu_w[l], sgu_b[l], bm=256)
        else:
            j = l - n_a
            if kv is None:
                kv = _mm_act(hb, w_kv_b, 0, _identity, _bf16, bm=bm_in, bn=bn_in)
            nj = mix_in_b.shape[2] // bn_in
            q, w_out = _mm_act(hb, w_in_b, j, lambda t: t * scale, _bf16, bm=bm_in, bn=bn_in,
                               side_casts=[slab_cast(mix_out_b, j, nj)])
            mix_in = _attention(q, kv, sinks[j], batch, seq)
        hf, hb = _mm_res_ln(mix_in, w_out, hf, ln_g[l, 0], ln_b[l, 0], alpha, bm=256)
        outs = _ffn(hb, hf, w_up, w_down, ffn_conv_w[l], ffn_conv_b[l], ln_g[l, 1], ln_b[l, 1],
                    alpha, seq, bm=512, bf=256, nres=min(8, ffn_down.shape[1] // 256),
                    up_stack=ffn_up, down_stack=ffn_down,
                    next_layer=l + 1 if l + 1 < depth else None)
        hf, hb = outs[:2]
        if l + 1 < depth:
            w_up, w_down = outs[2:]
    return hf.reshape(batch, seq, d)
```

```python
import functools

import jax
import jax.numpy as jnp
import numpy as np
from jax import lax
from jax.experimental import pallas as pl
from jax.experimental.pallas import tpu as pltpu

LN_EPS = 1e-5
CHUNK = 128
HEAD_DIM = 64
N_KV = 8
WINDOW = 128
CONV_W = 3

VMEM_LIMIT_BYTES = 56 * 1024 * 1024
SUBLANES = 8
LN_ROWS = 16
LN_UNROLL = 8
LN_COLS = 1024
CONV_TILE_ROWS = 32

_f32 = jnp.float32
_bf16 = jnp.bfloat16


def _params(sem):
    return pltpu.CompilerParams(dimension_semantics=sem, vmem_limit_bytes=VMEM_LIMIT_BYTES)


def _layer_norm(y, g, b):
    mu = jnp.mean(y, axis=-1, keepdims=True)
    yc = y - mu
    var = jnp.mean(yc * yc, axis=-1, keepdims=True)
    return yc * lax.rsqrt(var + LN_EPS) * g + b


def _layer_norm_rows(o_ref, ob_ref, gb_ref, stat_ref):
    ngroups = o_ref.shape[0] // LN_ROWS

    def group(r):
        return pl.ds(pl.multiple_of(r * LN_ROWS, LN_ROWS), LN_ROWS)

    def mean_pass(r, carry):
        rows = group(r)
        stat_ref[0, rows, :] = jnp.mean(o_ref[rows, :], axis=-1, keepdims=True)
        return carry

    n = o_ref.shape[1]
    slabs = [slice(c, c + LN_COLS) for c in range(0, n, LN_COLS)]

    def var_pass(r, carry):
        rows = group(r)
        mu = stat_ref[0, rows, :]
        sq = None
        for cols in slabs:
            yc = o_ref[rows, cols] - mu
            sq = yc * yc if sq is None else sq + yc * yc
        ssq = jnp.sum(sq, axis=-1, keepdims=True)
        stat_ref[1, rows, :] = lax.rsqrt(ssq * (1.0 / n) + LN_EPS)
        return carry

    def norm_pass(r, carry):
        rows = group(r)
        mu = stat_ref[0, rows, :]
        inv = stat_ref[1, rows, :]
        for cols in slabs:
            h = (o_ref[rows, cols] - mu) * inv * gb_ref[0:1, cols] + gb_ref[1:2, cols]
            o_ref[rows, cols] = h
            ob_ref[rows, cols] = h.astype(_bf16)
        return carry

    for one_pass in (mean_pass, var_pass, norm_pass):
        lax.fori_loop(0, ngroups, one_pass, 0, unroll=LN_UNROLL)


def _mm_act_kernel(x_ref, w_ref, *refs, act, nside):
    o_ref = refs[nside]
    z = jnp.dot(x_ref[...], w_ref[...], preferred_element_type=_f32)
    o_ref[...] = act(z).astype(o_ref.dtype)
    for src_ref, dst_ref in zip(refs[:nside], refs[nside + 1:]):
        dst_ref[...] = src_ref[...].astype(dst_ref.dtype)


def _mm_act(x, w, layer, act, out_dtype, bm, bn, side_casts=()):
    m, k = x.shape
    n = w.shape[2]
    outs = pl.pallas_call(
        functools.partial(_mm_act_kernel, act=act, nside=len(side_casts)),
        grid=(m // bm, n // bn),
        in_specs=[pl.BlockSpec((bm, k), lambda i, j: (i, 0)),
                  pl.BlockSpec((None, k, bn), lambda i, j: (layer, 0, j))]
                 + [sc[1] for sc in side_casts],
        out_specs=[pl.BlockSpec((bm, bn), lambda i, j: (i, j))] + [sc[3] for sc in side_casts],
        out_shape=[jax.ShapeDtypeStruct((m, n), out_dtype)]
                  + [jax.ShapeDtypeStruct(sc[2], _bf16) for sc in side_casts],
        compiler_params=_params(("arbitrary", "arbitrary")),
    )(x, w, *[sc[0] for sc in side_casts])
    return outs[0] if not side_casts else outs


def _row_tile_cast(src, layer, tile_rows, nj):
    _, r, c = src.shape
    last = r // tile_rows - 1
    assert r % tile_rows == 0

    def tile(i, j):
        return jnp.minimum(i * nj + j, last)

    return (src,
            pl.BlockSpec((None, tile_rows, c), lambda i, j: (layer, tile(i, j), 0)),
            (r, c),
            pl.BlockSpec((tile_rows, c), lambda i, j: (tile(i, j), 0)))


def _column_slab_cast(src, layer, rows, cols, slab):
    _, k, n = src.shape
    per_slab = slab // cols
    return (src,
            pl.BlockSpec((None, rows, cols), lambda i, j: (layer, i, j)),
            (n // slab, k, slab),
            pl.BlockSpec((None, rows, cols), lambda i, j: (j // per_slab, i, j % per_slab)))


def _cast_kernel(x_ref, o_ref):
    o_ref[...] = x_ref[...].astype(o_ref.dtype)


def _cast_bf16(w, block_rows):
    c = w.shape[-1]
    w2 = w.reshape(-1, c)
    r = w2.shape[0]
    assert r % block_rows == 0
    out = pl.pallas_call(
        _cast_kernel,
        grid=(r // block_rows,),
        in_specs=[pl.BlockSpec((block_rows, c), lambda i: (i, 0))],
        out_specs=pl.BlockSpec((block_rows, c), lambda i: (i, 0)),
        out_shape=jax.ShapeDtypeStruct((r, c), _bf16),
        compiler_params=_params(("arbitrary",)),
    )(w2)
    return out.reshape(w.shape)


def _mm_res_ln_kernel(x_ref, w_ref, res_ref, g_ref, b_ref, o_ref, ob_ref, y0_ref, y1_ref,
                      *, alpha, nslab):
    i = pl.program_id(0)
    j = pl.program_id(1)
    bn = w_ref.shape[2]
    rows_out = o_ref.shape[0]

    @pl.when((i == 0) & (j == 0))
    def _():
        y1_ref[...] = jnp.zeros_like(y1_ref)

    def step(y_new, y_done):
        y_new[j] = alpha * res_ref[...] + jnp.dot(x_ref[...], w_ref[j], preferred_element_type=_f32)
        for r in range(0, rows_out, LN_ROWS):
            rows = pl.ds(pl.multiple_of(j * rows_out + r, LN_ROWS), LN_ROWS)
            parts = [y_done[s, rows, :] for s in range(nslab)]
            mu = sum(jnp.sum(p, axis=-1, keepdims=True) for p in parts) * (1.0 / (nslab * bn))
            parts = [p - mu for p in parts]
            var = sum(jnp.sum(p * p, axis=-1, keepdims=True) for p in parts) * (1.0 / (nslab * bn))
            inv = lax.rsqrt(var + LN_EPS)
            for s in range(nslab):
                cols = slice(s * bn, (s + 1) * bn)
                h = parts[s] * inv * g_ref[:, cols] + b_ref[:, cols]
                o_ref[r:r + LN_ROWS, cols] = h
                ob_ref[r:r + LN_ROWS, cols] = h.astype(_bf16)

    @pl.when(i % 2 == 0)
    def _():
        step(y0_ref, y1_ref)

    @pl.when(i % 2 == 1)
    def _():
        step(y1_ref, y0_ref)


def _mm_res_ln(x, w_slabs, res, g, b, alpha, bm):
    m, k = x.shape
    nslab, _, bn = w_slabs.shape
    n = nslab * bn
    nblk = m // bm
    rows_out = bm // nslab
    assert rows_out % LN_ROWS == 0

    def block_in(i):
        return jnp.minimum(i, nblk - 1)

    def out_map(i, j):
        return (jnp.maximum(i - 1, 0) * nslab + jnp.where(i > 0, j, 0), 0)

    return pl.pallas_call(
        functools.partial(_mm_res_ln_kernel, alpha=alpha, nslab=nslab),
        grid=(nblk + 1, nslab),
        in_specs=[pl.BlockSpec((bm, k), lambda i, j: (block_in(i), 0)),
                  pl.BlockSpec((nslab, k, bn), lambda i, j: (0, 0, 0), pipeline_mode=pl.Buffered(1)),
                  pl.BlockSpec((bm, bn), lambda i, j: (block_in(i), j)),
                  pl.BlockSpec(memory_space=pltpu.VMEM),
                  pl.BlockSpec(memory_space=pltpu.VMEM)],
        out_specs=[pl.BlockSpec((rows_out, n), out_map),
                   pl.BlockSpec((rows_out, n), out_map)],
        out_shape=[jax.ShapeDtypeStruct((m, n), _f32),
                   jax.ShapeDtypeStruct((m, n), _bf16)],
        scratch_shapes=[pltpu.VMEM((nslab, bm, bn), _f32),
                        pltpu.VMEM((nslab, bm, bn), _f32)],
        compiler_params=_params(("arbitrary", "arbitrary")),
    )(x, w_slabs, res, g.reshape(1, n), b.reshape(1, n))


def _sgu_kernel(u_ref, v_ref, vg_ref, vb_ref, w_ref, bs_ref, o_ref, *, nchunk, ngroup):
    vn = _layer_norm(v_ref[...].astype(_f32), vg_ref[...], vb_ref[...]).astype(_bf16)
    row = lax.broadcasted_iota(jnp.int32, (CHUNK, CHUNK), 0)
    col = lax.broadcasted_iota(jnp.int32, (CHUNK, CHUNK), 1)
    causal = col <= row
    for g in range(ngroup):
        lanes = slice(g * CHUNK, (g + 1) * CHUNK)
        wg = jnp.where(causal, w_ref[g], 0.0).astype(_bf16)
        rhs = jnp.concatenate(
            [vn[c * CHUNK:(c + 1) * CHUNK, lanes] for c in range(nchunk)], axis=1)
        vm = jnp.dot(wg, rhs, preferred_element_type=_f32) + bs_ref[g]
        for c in range(nchunk):
            rows = slice(c * CHUNK, (c + 1) * CHUNK)
            o_ref[rows, lanes] = (u_ref[rows, lanes].astype(_f32)
                                  * vm[:, c * CHUNK:(c + 1) * CHUNK]).astype(_bf16)


def _sgu(z, vg, vb, w_s, b_s, bm):
    m, n2 = z.shape
    d = n2 // 2
    ngroup = d // CHUNK
    return pl.pallas_call(
        functools.partial(_sgu_kernel, nchunk=bm // CHUNK, ngroup=ngroup),
        grid=(m // bm,),
        in_specs=[pl.BlockSpec((bm, d), lambda i: (i, 0)),
                  pl.BlockSpec((bm, d), lambda i: (i, 1)),
                  pl.BlockSpec((1, d), lambda i: (0, 0)),
                  pl.BlockSpec((1, d), lambda i: (0, 0)),
                  pl.BlockSpec((ngroup, CHUNK, CHUNK), lambda i: (0, 0, 0)),
                  pl.BlockSpec((ngroup, CHUNK, 1), lambda i: (0, 0, 0))],
        out_specs=pl.BlockSpec((bm, d), lambda i: (i, 0)),
        out_shape=jax.ShapeDtypeStruct((m, d), _bf16),
        compiler_params=_params(("arbitrary",)),
    )(z, z, vg.reshape(1, d), vb.reshape(1, d), w_s, b_s.reshape(ngroup, CHUNK, 1))


def _attn_kernel(sink_ref, q_ref, kp_ref, kc_ref, vp_ref, vc_ref, *refs, gqa, nside):
    o_ref = refs[nside]
    for src_ref, dst_ref in zip(refs[:nside], refs[nside + 1:]):
        dst_ref[...] = src_ref[...].astype(dst_ref.dtype)
    n = pl.program_id(1)
    blk = q_ref.shape[0]
    npair = gqa // 2
    pair_w = 2 * HEAD_DIM
    lower_kv = lax.broadcasted_iota(jnp.int32, (2 * blk, pair_w), 1) < HEAD_DIM
    key0 = lax.broadcasted_iota(jnp.int32, (2 * blk, pair_w), 0) == 0
    lower_q = lax.broadcasted_iota(jnp.int32, (blk, pair_w), 1) < HEAD_DIM
    qi = lax.broadcasted_iota(jnp.int32, (blk, 2 * blk), 0)
    kj = lax.broadcasted_iota(jnp.int32, (blk, 2 * blk), 1)
    valid = (kj > qi) & (kj <= qi + WINDOW) & ((n > 0) | (kj >= blk))
    bias = jnp.where(valid, 0.0, -jnp.inf)
    sink_col = kj == 0

    def halves(prev_ref, cur_ref, hp, fill):
        lanes = slice(hp * pair_w, (hp + 1) * pair_w)
        both = jnp.concatenate([prev_ref[:, lanes], cur_ref[:, lanes]], axis=0).astype(_f32)
        both = jnp.where(key0, 0.0, both)
        swapped = pltpu.roll(both, HEAD_DIM, 1)
        pad = jnp.full_like(both, fill)
        head_a = (jnp.where(lower_kv, both, pad), jnp.where(lower_kv, pad, swapped))
        head_b = (jnp.where(lower_kv, swapped, pad), jnp.where(lower_kv, pad, both))
        return [tuple(t.astype(_bf16) for t in head_a), tuple(t.astype(_bf16) for t in head_b)]

    for hp in range(N_KV // 2):
        k_heads = halves(kp_ref, kc_ref, hp, 0.0)
        v_heads = halves(vp_ref, vc_ref, hp, 1.0)
        for sub in range(2):
            h = 2 * hp + sub
            base = h * gqa * HEAD_DIM
            qs = jnp.concatenate(
                [q_ref[:, base + p * pair_w: base + (p + 1) * pair_w] for p in range(npair)], axis=0)
            pv = []
            for slot in range(2):
                s_all = lax.dot_general(qs, k_heads[sub][slot], (((1,), (1,)), ((), ())),
                                        preferred_element_type=_f32)
                e_slabs = []
                for p in range(npair):
                    sink = sink_ref[h * gqa + 2 * p + slot]
                    s = s_all[p * blk:(p + 1) * blk] + jnp.where(sink_col, sink, bias)
                    mx = jnp.max(s, axis=-1, keepdims=True)
                    e_slabs.append(jnp.exp(s - mx).astype(_bf16))
                pv.append(jnp.dot(jnp.concatenate(e_slabs, axis=0), v_heads[sub][slot],
                                  preferred_element_type=_f32))
            for p in range(npair):
                a = pv[0][p * blk:(p + 1) * blk]
                b = pv[1][p * blk:(p + 1) * blk]
                num = jnp.where(lower_q, a, b)
                den = pltpu.roll(jnp.where(lower_q, b, a), HEAD_DIM, 1)
                o_ref[:, base + p * pair_w: base + (p + 1) * pair_w] = (num / den).astype(o_ref.dtype)


def _attention(q, kv, sinks, batch, seq, side_casts=()):
    m, d = q.shape
    blk = WINDOW
    nb = seq // blk
    kvw = N_KV * HEAD_DIM
    gqa = d // (N_KV * HEAD_DIM)

    def cur(col):
        return lambda b, n: (b * nb + n, col)

    def prev(col):
        return lambda b, n: (b * nb + jnp.maximum(n - 1, 0), col)

    outs = pl.pallas_call(
        functools.partial(_attn_kernel, gqa=gqa, nside=len(side_casts)),
        grid=(batch, nb),
        in_specs=[pl.BlockSpec(memory_space=pltpu.SMEM),
                  pl.BlockSpec((blk, d), cur(0)),
                  pl.BlockSpec((blk, kvw), prev(0)),
                  pl.BlockSpec((blk, kvw), cur(0)),
                  pl.BlockSpec((blk, kvw), prev(1)),
                  pl.BlockSpec((blk, kvw), cur(1))] + [sc[1] for sc in side_casts],
        out_specs=[pl.BlockSpec((blk, d), cur(0))] + [sc[3] for sc in side_casts],
        out_shape=[jax.ShapeDtypeStruct((m, d), _bf16)]
                  + [jax.ShapeDtypeStruct(sc[2], _bf16) for sc in side_casts],
        compiler_params=_params(("arbitrary", "arbitrary")),
    )(sinks, q, kv, kv, kv, kv, *[sc[0] for sc in side_casts])
    return outs[0] if not side_casts else outs


def _ffn_kernel(ia_ref, fa_ref, fb_ref, ic_ref, fc_ref, start_ref,
                hb_ref, hf_ref, wg_ref, wu_ref, cp_ref, wd_ref, gb_ref, *refs,
                alpha, nf, nres, cast_next):
    if cast_next:
        up_next_ref, down_next_ref = refs[:2]
        o_ref, ob_ref, up_cast_ref, down_cast_ref = refs[2:6]
        zs0_ref, zs1_ref, act0_ref, act1_ref, tail_ref, stat_ref = refs[6:]
    else:
        o_ref, ob_ref = refs[:2]
        zs0_ref, zs1_ref, act0_ref, act1_ref, tail_ref, stat_ref = refs[2:]
    t = pl.program_id(0)
    bm = hb_ref.shape[0]
    res_rows = hf_ref.shape[0]
    fa = fa_ref[t]
    fb = fb_ref[t]
    seq_start = start_ref[t] == 1
    fc = fc_ref[t]

    @pl.when(t == 0)
    def _():
        zs0_ref[...] = jnp.zeros_like(zs0_ref)
        zs1_ref[...] = jnp.zeros_like(zs1_ref)
        act0_ref[...] = jnp.zeros_like(act0_ref)
        act1_ref[...] = jnp.zeros_like(act1_ref)
        tail_ref[...] = jnp.zeros_like(tail_ref)

    @pl.when(fc == 0)
    def _():
        o_ref[...] = jnp.zeros_like(o_ref)

    bf = wg_ref.shape[1]

    def conv_tile(zs_ref, half, r):
        rows = CONV_TILE_ROWS
        lanes = slice(half * bf, (half + 1) * bf)
        z0 = zs_ref[half, SUBLANES + r:SUBLANES + r + rows, :]
        z1 = zs_ref[half, SUBLANES - 1 + r:SUBLANES - 1 + r + rows, :]
        z2 = zs_ref[half, SUBLANES - 2 + r:SUBLANES - 2 + r + rows, :]
        return (cp_ref[fb, 0:1, lanes] * z2 + cp_ref[fb, 1:2, lanes] * z1
                + cp_ref[fb, 2:3, lanes] * z0 + cp_ref[fb, 3:4, lanes])

    def stages(zs_a, zs_b, act_b, act_c):
        hb = hb_ref[...]
        for half, w_ref in ((0, wg_ref), (1, wu_ref)):
            zs_a[half, :SUBLANES, :] = jnp.where(seq_start, 0.0, tail_ref[fa, half])
            zs_a[half, SUBLANES:, :] = jnp.dot(hb, w_ref[...], preferred_element_type=_f32)
            tail_ref[fa, half] = zs_a[half, bm:, :]
        ntile = bm // CONV_TILE_ROWS
        bn = o_ref.shape[1] // ntile
        for k in range(ntile):
            cols = slice(k * bn, (k + 1) * bn)
            o_ref[:, cols] += jnp.dot(act_c[...], wd_ref[:, cols], preferred_element_type=_f32)
            r = k * CONV_TILE_ROWS
            gate = conv_tile(zs_b, 0, r)
            up = conv_tile(zs_b, 1, r)
            act_b[r:r + CONV_TILE_ROWS, :] = (gate * jax.nn.sigmoid(gate) * up).astype(_bf16)
        if cast_next:
            up_cast_ref[...] = up_next_ref[...].astype(_bf16)
            down_cast_ref[...] = down_next_ref[...].astype(_bf16)

    @pl.when(t % 2 == 0)
    def _():
        stages(zs0_ref, zs1_ref, act1_ref, act0_ref)

    @pl.when(t % 2 == 1)
    def _():
        stages(zs1_ref, zs0_ref, act0_ref, act1_ref)

    @pl.when((fc < nres) & (t >= 2))
    def _():
        rows = pl.ds(pl.multiple_of(fc * res_rows, res_rows), res_rows)
        o_ref[rows, :] += alpha * hf_ref[...]

    @pl.when((fc == nf - 1) & (t >= 2))
    def _():
        _layer_norm_rows(o_ref, ob_ref, gb_ref, stat_ref)


def _ffn(hb, hf, w_up, w_down, conv_w, conv_b, g, b, alpha, seq, bm, bf, nres,
         up_stack=None, down_stack=None, next_layer=None):
    m, d = hb.shape
    d_ff = w_down.shape[0]
    nf = d_ff // bf
    nblk = m // bm
    nchunks = nblk * nf
    res_rows = bm // nres
    cast_next = next_layer is not None
    assert nres <= nf and seq % bm == 0 and bm % CONV_TILE_ROWS == 0 and res_rows % SUBLANES == 0
    up_rows = down_cols = d // nblk
    assert up_rows % LN_ROWS == 0 and down_cols % 128 == 0
    taps = jnp.concatenate([conv_w, conv_b[None, :]], axis=0).reshape(CONV_W + 1, 2, nf, bf)
    conv_params = jnp.pad(taps.transpose(2, 0, 1, 3).reshape(nf, CONV_W + 1, 2 * bf),
                          ((0, 0), (0, SUBLANES - CONV_W - 1), (0, 0)))
    gain_bias = jnp.stack([g, b])

    steps = np.arange(nchunks + 2)
    chunk_a = np.minimum(steps, nchunks - 1)
    chunk_b = np.clip(steps - 1, 0, nchunks - 1)
    chunk_c = np.maximum(steps - 2, 0)
    tables = [chunk_a // nf, chunk_a % nf, chunk_b % nf, chunk_c // nf, chunk_c % nf,
              ((chunk_a // nf) % (seq // bm) == 0).astype(np.int32)]
    tables = [jnp.asarray(tbl, jnp.int32) for tbl in tables]

    def res_map(t, ia, fa, fb, ic, fc, start):
        return (ic[t] * nres + jnp.minimum(fc[t], nres - 1), 0)

    in_specs = [pl.BlockSpec((bm, d), lambda t, ia, fa, fb, ic, fc, start: (ia[t], 0)),
                pl.BlockSpec((res_rows, d), res_map),
                pl.BlockSpec((d, bf), lambda t, ia, fa, fb, ic, fc, start: (0, fa[t])),
                pl.BlockSpec((d, bf), lambda t, ia, fa, fb, ic, fc, start: (0, nf + fa[t])),
                pl.BlockSpec(memory_space=pltpu.VMEM),
                pl.BlockSpec((bf, d), lambda t, ia, fa, fb, ic, fc, start: (fc[t], 0)),
                pl.BlockSpec(memory_space=pltpu.VMEM)]
    out_specs = [pl.BlockSpec((bm, d), lambda t, ia, fa, fb, ic, fc, start: (ic[t], 0)),
                 pl.BlockSpec((bm, d), lambda t, ia, fa, fb, ic, fc, start: (ic[t], 0))]
    out_shape = [jax.ShapeDtypeStruct((m, d), _f32), jax.ShapeDtypeStruct((m, d), _bf16)]
    operands = [hb, hf, w_up, w_up, conv_params, w_down, gain_bias]
    if cast_next:
        in_specs += [pl.BlockSpec((None, up_rows, 2 * bf),
                                  lambda t, ia, fa, fb, ic, fc, start: (next_layer, ia[t], fa[t])),
                     pl.BlockSpec((None, bf, down_cols),
                                  lambda t, ia, fa, fb, ic, fc, start: (next_layer, fa[t], ia[t]))]
        out_specs += [pl.BlockSpec((up_rows, 2 * bf),
                                   lambda t, ia, fa, fb, ic, fc, start: (ia[t], fa[t])),
                      pl.BlockSpec((bf, down_cols),
                                   lambda t, ia, fa, fb, ic, fc, start: (fa[t], ia[t]))]
        out_shape += [jax.ShapeDtypeStruct((d, 2 * d_ff), _bf16),
                      jax.ShapeDtypeStruct((d_ff, d), _bf16)]
        operands += [up_stack, down_stack]

    grid_spec = pltpu.PrefetchScalarGridSpec(
        num_scalar_prefetch=len(tables),
        grid=(nchunks + 2,),
        in_specs=in_specs,
        out_specs=out_specs,
        scratch_shapes=[pltpu.VMEM((2, SUBLANES + bm, bf), _f32),
                        pltpu.VMEM((2, SUBLANES + bm, bf), _f32),
                        pltpu.VMEM((bm, bf), _bf16),
                        pltpu.VMEM((bm, bf), _bf16),
                        pltpu.VMEM((nf, 2, SUBLANES, bf), _f32),
                        pltpu.VMEM((2, bm, 1), _f32)])
    return pl.pallas_call(
        functools.partial(_ffn_kernel, alpha=alpha, nf=nf, nres=nres, cast_next=cast_next),
        grid_spec=grid_spec,
        out_shape=out_shape,
        compiler_params=_params(("arbitrary",)),
    )(*tables, *operands)


def _gelu(z):
    return 0.5 * z * (1.0 + lax.erf(z * (2.0 ** -0.5)))


def _identity(z):
    return z


def kernel(x, mix_in_a, norm_v_a_g, norm_v_a_b, sgu_w, sgu_b, mix_out_a, w_kv, mix_in_b, sinks,
           mix_out_b, ffn_up, ffn_conv_w, ffn_conv_b, ffn_down, ln_g, ln_b):
    batch, seq, d = x.shape
    depth = ffn_up.shape[0]
    n_a = mix_in_a.shape[0]
    alpha = (2.0 * depth) ** 0.25
    scale = HEAD_DIM ** -0.5
    m = batch * seq

    assert n_a >= 1
    bm_in, bn_in = 1024, 1024
    row_blocks = m // bm_in
    out_slab = min(1024, d)
    w_in_a = _cast_bf16(mix_in_a, block_rows=256)

    def row_tiles(src, steps):
        r = src.shape[1]
        return next(t for t in range(LN_ROWS, r + 1, LN_ROWS) if r % t == 0 and r // t <= steps)

    def slab_cast(src, layer, nj):
        return _column_slab_cast(src, layer, src.shape[1] // row_blocks, src.shape[2] // nj, out_slab)

    hf = x.reshape(m, d)
    hb = hf.astype(_bf16)
    kv = None
    for l in range(depth):
        if l < n_a:
            bn_a = bn_in // 2 if l == 0 else bn_in
            nj = mix_in_a.shape[2] // bn_a
            steps = row_blocks * nj
            sides = [slab_cast(mix_out_a, l, nj)]
            if l == 0:
                sides += [_row_tile_cast(ffn_up, 0, row_tiles(ffn_up, steps), nj),
                          _row_tile_cast(ffn_down, 0, row_tiles(ffn_down, steps), nj)]
            if l == n_a - 1:
                q_stack = mix_in_b.reshape(1, -1, mix_in_b.shape[2])
                kv_stack = w_kv[None]
                sides += [_row_tile_cast(q_stack, 0, row_tiles(q_stack, steps), nj),
                          _row_tile_cast(kv_stack, 0, row_tiles(kv_stack, steps), nj)]
            outs = _mm_act(hb, w_in_a, l, _gelu, _bf16, bm=bm_in, bn=bn_a, side_casts=sides)
            z, w_out, rest = outs[0], outs[1], list(outs[2:])
            if l == 0:
                w_up, w_down = rest[:2]
                rest = rest[2:]
            if l == n_a - 1:
                w_in_b = rest[0].reshape(mix_in_b.shape)
                w_kv_b = rest[1][None]
            mix_in = _sgu(z, norm_v_a_g[l], norm_v_a_b[l], sgu_w[l], sgu_b[l], bm=256)
        else:
            j = l - n_a
            if kv is None:
                kv = _mm_act(hb, w_kv_b, 0, _identity, _bf16, bm=bm_in, bn=bn_in)
            nj = mix_in_b.shape[2] // bn_in
            q, w_out = _mm_act(hb, w_in_b, j, lambda t: t * scale, _bf16, bm=bm_in, bn=bn_in,
                               side_casts=[slab_cast(mix_out_b, j, nj)])
            attn_steps = m // WINDOW
            mix_in, w_up, w_down = _attention(
                q, kv, sinks[j], batch, seq,
                side_casts=[_row_tile_cast(ffn_up, l, row_tiles(ffn_up, attn_steps), seq // WINDOW),
                            _row_tile_cast(ffn_down, l, row_tiles(ffn_down, attn_steps), seq // WINDOW)])
        hf, hb = _mm_res_ln(mix_in, w_out, hf, ln_g[l, 0], ln_b[l, 0], alpha, bm=256)
        cast_next = l + 1 < n_a
        outs = _ffn(hb, hf, w_up, w_down, ffn_conv_w[l], ffn_conv_b[l], ln_g[l, 1], ln_b[l, 1],
                    alpha, seq, bm=512, bf=256, nres=min(8, ffn_down.shape[1] // 256),
                    up_stack=ffn_up, down_stack=ffn_down,
                    next_layer=l + 1 if cast_next else None)
        hf, hb = outs[:2]
        if cast_next:
            w_up, w_down = outs[2:]
    return hf.reshape(batch, seq, d)
```
